```python
import math
import jax, jax.numpy as jnp
from jax import lax
import numpy as np

D_MODEL = 2048
BATCH = 8
SEQ = 4096
DEPTH = 2

F32 = jnp.float32
NORM_EPS = 1e-6
PLE_DIM = 256
N_BRANCH = 4
BRANCH_WIDTH = D_MODEL // 2
CONV_WIDTH = 4
SSM_HEAD_DIM = 64
SSM_HEADS = BRANCH_WIDTH // SSM_HEAD_DIM
SSM_GROUPS = 4
SSM_STATE = 128
SSM_CHUNK = 128
SSM_CONV_DIM = BRANCH_WIDTH + 2 * SSM_GROUPS * SSM_STATE
HGRN_HEAD_DIM = 128
HGRN_HEADS = BRANCH_WIDTH // HGRN_HEAD_DIM
HGRN_CHUNK = 64
MLSTM_HEADS = 4
MLSTM_QK_DIM = BRANCH_WIDTH // (2 * MLSTM_HEADS)
MLSTM_V_DIM = BRANCH_WIDTH // MLSTM_HEADS
MLSTM_CHUNK = 64
LRU_BLOCKS = 8
LRU_BLOCK_DIM = BRANCH_WIDTH // LRU_BLOCKS
LRU_C = 8.0
D_FF = 4 * D_MODEL
IN_WIDTHS = (
    BRANCH_WIDTH, SSM_CONV_DIM, SSM_HEADS,
    BRANCH_WIDTH, BRANCH_WIDTH, BRANCH_WIDTH, BRANCH_WIDTH,
    MLSTM_HEADS * MLSTM_QK_DIM, MLSTM_HEADS * MLSTM_QK_DIM,
    BRANCH_WIDTH, BRANCH_WIDTH, MLSTM_HEADS, MLSTM_HEADS,
    BRANCH_WIDTH, BRANCH_WIDTH,
    N_BRANCH * D_MODEL,
)
D_IN = sum(IN_WIDTHS)

kernel_name = 'hybrid_gated_ssd_hgrn2_mlstm_rglru'


def rmsnorm(x, w):
    xf = x.astype(F32)
    y = xf * lax.rsqrt(jnp.mean(xf * xf, axis=-1, keepdims=True) + NORM_EPS)
    return (y * w.astype(F32)).astype(x.dtype)


def causal_dwconv(x, w, b):
    width, ch = w.shape
    y = lax.conv_general_dilated(x, w[:, None, :].astype(x.dtype), window_strides=(1,),
                                 padding=[(width - 1, 0)], dimension_numbers=('NWC', 'WIO', 'NWC'),
                                 feature_group_count=ch)
    return y + b.astype(x.dtype)


def segsum(a):
    t = a.shape[-1]
    cs = jnp.cumsum(a, axis=-1)
    mask = jnp.tril(jnp.ones((t, t), dtype=bool))
    return jnp.where(mask, cs[..., :, None] - cs[..., None, :], -jnp.inf)


def to_chunks(t, chunk):
    b, l, h = t.shape[:3]
    t = t.reshape(b, l // chunk, chunk, h, *t.shape[3:])
    return jnp.moveaxis(t, (1, 3), (0, 2))


def from_chunks(t):
    nc, b, h, c = t.shape[:4]
    t = jnp.moveaxis(t, (0, 2), (1, 3))
    return t.reshape(b, nc * c, h, *t.shape[4:])


def ssd_chunked(xs, dt, a, bm, cm):
    b, l, h, p = xs.shape
    g, n = bm.shape[2], bm.shape[3]
    r = h // g
    c = l // SSM_CHUNK
    x_dt = (xs * dt[..., None]).reshape(b, c, SSM_CHUNK, g, r, p)
    a_dt = jnp.transpose((dt * a).reshape(b, c, SSM_CHUNK, g, r), (0, 3, 4, 1, 2))
    bc = bm.reshape(b, c, SSM_CHUNK, g, n)
    cc = cm.reshape(b, c, SSM_CHUNK, g, n)
    a_cs = jnp.cumsum(a_dt, axis=-1)
    y_diag = jnp.einsum('bclgn,bcsgn,bgrcls,bcsgrp->bclgrp', cc, bc, jnp.exp(segsum(a_dt)), x_dt)
    decay_states = jnp.exp(a_cs[..., -1:] - a_cs)
    states = jnp.einsum('bcsgn,bgrcs,bcsgrp->bcgrpn', bc, decay_states, x_dt)
    states = jnp.concatenate([jnp.zeros_like(states[:, :1]), states], axis=1)
    chunk_decay = jnp.exp(segsum(jnp.pad(a_cs[..., -1], [(0, 0)] * 3 + [(1, 0)])))
    states = jnp.einsum('bgrzc,bcgrpn->bzgrpn', chunk_decay, states)[:, :-1]
    y_off = jnp.einsum('bclgn,bcgrpn,bgrcl->bclgrp', cc, states, jnp.exp(a_cs))
    return (y_diag + y_off).reshape(b, l, h, p)


def mamba2_mixer(z, xbc, dt_pre, conv_w, conv_b, dt_bias, a_log, d_skip, norm_w):
    b, l, _ = z.shape
    xbc = jax.nn.silu(causal_dwconv(xbc, conv_w, conv_b)).astype(F32)
    xs, bm, cm = jnp.split(xbc, [BRANCH_WIDTH, BRANCH_WIDTH + SSM_GROUPS * SSM_STATE], axis=-1)
    xs = xs.reshape(b, l, SSM_HEADS, SSM_HEAD_DIM)
    bm = bm.reshape(b, l, SSM_GROUPS, SSM_STATE)
    cm = cm.reshape(b, l, SSM_GROUPS, SSM_STATE)
    dt = jax.nn.softplus(dt_pre.astype(F32) + dt_bias.astype(F32))
    a = -jnp.exp(a_log.astype(F32))
    y = ssd_chunked(xs, dt, a, bm, cm) + d_skip.astype(F32)[:, None] * xs
    y = y.reshape(b, l, BRANCH_WIDTH) * jax.nn.silu(z.astype(F32))
    y = rmsnorm(y.reshape(b, l, SSM_GROUPS, -1), norm_w.reshape(SSM_GROUPS, -1))
    return y.reshape(b, l, BRANCH_WIDTH)


def gla_chunk_step(s, inp):
    q, k, v, g = inp
    c = q.shape[2]
    gc = jnp.cumsum(g, axis=2)
    causal = jnp.tril(jnp.ones((c, c), dtype=bool))
    decay = jnp.exp(jnp.where(causal[:, :, None], gc[:, :, :, None, :] - gc[:, :, None, :, :], -jnp.inf))
    attn = jnp.einsum('bhid,bhjd,bhijd->bhij', q, k, decay)
    o = attn @ v + jnp.einsum('bhid,bhde->bhie', q * jnp.exp(gc), s)
    g_last = gc[:, :, -1]
    s = jnp.exp(g_last)[..., None] * s + jnp.einsum('bhjd,bhje->bhde', k * jnp.exp(g_last[:, :, None] - gc), v)
    return s, o


def hgrn2_mixer(q_pre, f_pre, i_in, g_pre, lb, norm_w):
    b, l, _ = q_pre.shape
    shp = (b, l, HGRN_HEADS, HGRN_HEAD_DIM)
    f_pre = f_pre.astype(F32)
    lb = lb.astype(F32)
    q = jax.nn.silu(q_pre.astype(F32)) * HGRN_HEAD_DIM ** -0.5
    log_f = jnp.logaddexp(jnp.log(lb), jnp.log1p(-lb) + jax.nn.log_sigmoid(f_pre))
    k = (1.0 - lb) * jax.nn.sigmoid(-f_pre)
    xs = tuple(to_chunks(t.reshape(shp), HGRN_CHUNK) for t in (q, k, i_in.astype(F32), log_f))
    s0 = jnp.zeros((b, HGRN_HEADS, HGRN_HEAD_DIM, HGRN_HEAD_DIM), F32)
    _, o = lax.scan(gla_chunk_step, s0, xs)
    o = rmsnorm(from_chunks(o), norm_w.reshape(HGRN_HEADS, HGRN_HEAD_DIM)).reshape(b, l, BRANCH_WIDTH)
    return o * jax.nn.silu(g_pre.astype(F32))


def mlstm_chunk_step(carry, inp):
    c_st, n_st, m_st = carry
    q, k, v, ig, lf = inp
    c = q.shape[2]
    bcum = jnp.cumsum(lf, axis=-1)
    causal = jnp.tril(jnp.ones((c, c), dtype=bool))
    dlog = jnp.where(causal, bcum[..., :, None] - bcum[..., None, :] + ig[..., None, :], -jnp.inf)
    inter_log = bcum + m_st[..., None]
    m = jnp.maximum(inter_log, jnp.max(dlog, axis=-1))
    w_intra = jnp.exp(dlog - m[..., None])
    w_inter = jnp.exp(inter_log - m)
    s = jnp.einsum('bhid,bhjd->bhij', q, k) * w_intra
    num = s @ v + w_inter[..., None] * jnp.einsum('bhid,bhde->bhie', q, c_st)
    den = jnp.sum(s, axis=-1) + w_inter * jnp.einsum('bhid,bhd->bhi', q, n_st)
    h = num / jnp.maximum(jnp.abs(den), jnp.exp(-m))[..., None]
    b_last = bcum[..., -1]
    log_w = b_last[..., None] - bcum + ig
    m_new = jnp.maximum(b_last + m_st, jnp.max(log_w, axis=-1))
    wk = jnp.exp(log_w - m_new[..., None])
    decay = jnp.exp(b_last + m_st - m_new)
    c_st = decay[..., None, None] * c_st + jnp.einsum('bhj,bhjd,bhje->bhde', wk, k, v)
    n_st = decay[..., None] * n_st + jnp.einsum('bhj,bhjd->bhd', wk, k)
    return (c_st, n_st, m_new), h


def mlstm_mixer(q_pre, k_pre, v_in, o_pre, i_pre, f_pre, i_bias, f_bias, norm_w):
    b, l, _ = q_pre.shape
    q = q_pre.astype(F32).reshape(b, l, MLSTM_HEADS, MLSTM_QK_DIM) * MLSTM_QK_DIM ** -0.5
    k = k_pre.astype(F32).reshape(b, l, MLSTM_HEADS, MLSTM_QK_DIM)
    v = v_in.astype(F32).reshape(b, l, MLSTM_HEADS, MLSTM_V_DIM)
    ig = i_pre.astype(F32) + i_bias.astype(F32)
    lf = jax.nn.log_sigmoid(f_pre.astype(F32) + f_bias.astype(F32))
    xs = tuple(to_chunks(t, MLSTM_CHUNK) for t in (q, k, v, ig, lf))
    carry0 = (jnp.zeros((b, MLSTM_HEADS, MLSTM_QK_DIM, MLSTM_V_DIM), F32),
              jnp.zeros((b, MLSTM_HEADS, MLSTM_QK_DIM), F32),
              jnp.zeros((b, MLSTM_HEADS), F32))
    _, h = lax.scan(mlstm_chunk_step, carry0, xs)
    h = rmsnorm(from_chunks(h), norm_w.reshape(MLSTM_HEADS, MLSTM_V_DIM)).reshape(b, l, BRANCH_WIDTH)
    return h * jax.nn.sigmoid(o_pre.astype(F32))


def lru_combine(left, right):
    a1, b1 = left
    a2, b2 = right
    return a1 * a2, a2 * b1 + b2


def rglru_mixer(x_in, gate_in, conv_w, conv_b, wa, ba, wi, bi, a_param):
    b, l, _ = x_in.shape
    xc = causal_dwconv(x_in, conv_w, conv_b).astype(F32)
    xb = xc.reshape(b, l, LRU_BLOCKS, LRU_BLOCK_DIM)
    r = jax.nn.sigmoid(jnp.einsum('blnc,ncd->blnd', xb, wa.astype(F32)).reshape(b, l, BRANCH_WIDTH) + ba.astype(F32))
    i = jax.nn.sigmoid(jnp.einsum('blnc,ncd->blnd', xb, wi.astype(F32)).reshape(b, l, BRANCH_WIDTH) + bi.astype(F32))
    log_a = -LRU_C * r * jax.nn.softplus(-a_param.astype(F32))
    u = xc * i * jnp.sqrt(-jnp.expm1(2.0 * log_a))
    _, h = lax.associative_scan(lru_combine, (jnp.exp(log_a), u), axis=1)
    return h * jax.nn.gelu(gate_in.astype(F32))


def setup_inputs(seed: int = 0) -> dict:
    key = jax.random.key(seed)
    k = jax.random.split(key, 32)
    nrm = lambda kk, shape, scale: scale * jax.random.normal(kk, shape, F32)
    gain = lambda kk, shape: 1.0 + 0.02 * jax.random.normal(kk, shape, F32)
    dt = jnp.exp(jax.random.uniform(k[6], (DEPTH, SSM_HEADS), F32, math.log(1e-3), math.log(1e-1)))
    u = jax.random.uniform(k[21], (DEPTH, BRANCH_WIDTH), F32, 0.9, 0.999)
    s = u ** (1.0 / LRU_C)
    return {
        'x': nrm(k[0], (BATCH, SEQ, D_MODEL), 1.0),
        'p': nrm(k[1], (DEPTH, BATCH, SEQ, PLE_DIM), 1.0),
        'mix_norm': gain(k[2], (DEPTH, D_MODEL)),
        'w_in': nrm(k[3], (DEPTH, D_MODEL, D_IN), D_MODEL ** -0.5),
        'ssm_conv_w': nrm(k[4], (DEPTH, CONV_WIDTH, SSM_CONV_DIM), CONV_WIDTH ** -0.5),
        'ssm_conv_b': nrm(k[5], (DEPTH, SSM_CONV_DIM), 0.02),
        'ssm_dt_bias': dt + jnp.log(-jnp.expm1(-dt)),
        'ssm_a_log': jnp.log(jax.random.uniform(k[7], (DEPTH, SSM_HEADS), F32, 1.0, 16.0)),
        'ssm_d': gain(k[8], (DEPTH, SSM_HEADS)),
        'ssm_norm': gain(k[9], (DEPTH, BRANCH_WIDTH)),
        'hgrn_lb_logits': nrm(k[10], (DEPTH, BRANCH_WIDTH), 0.1),
        'hgrn_norm': gain(k[11], (DEPTH, BRANCH_WIDTH)),
        'mlstm_i_bias': -1.0 + nrm(k[12], (DEPTH, MLSTM_HEADS), 0.1),
        'mlstm_f_bias': jnp.linspace(3.0, 6.0, MLSTM_HEADS, dtype=F32)[None, :] + nrm(k[13], (DEPTH, MLSTM_HEADS), 0.1),
        'mlstm_norm': gain(k[14], (DEPTH, BRANCH_WIDTH)),
        'lru_conv_w': nrm(k[15], (DEPTH, CONV_WIDTH, BRANCH_WIDTH), CONV_WIDTH ** -0.5),
        'lru_conv_b': nrm(k[16], (DEPTH, BRANCH_WIDTH), 0.02),
        'lru_wa': nrm(k[17], (DEPTH, LRU_BLOCKS, LRU_BLOCK_DIM, LRU_BLOCK_DIM), LRU_BLOCK_DIM ** -0.5),
        'lru_ba': nrm(k[18], (DEPTH, BRANCH_WIDTH), 0.02),
        'lru_wi': nrm(k[19], (DEPTH, LRU_BLOCKS, LRU_BLOCK_DIM, LRU_BLOCK_DIM), LRU_BLOCK_DIM ** -0.5),
        'lru_bi': nrm(k[20], (DEPTH, BRANCH_WIDTH), 0.02),
        'lru_a_param': jnp.log(s) - jnp.log1p(-s),
        'w_branch': nrm(k[22], (DEPTH, N_BRANCH, BRANCH_WIDTH, D_MODEL), BRANCH_WIDTH ** -0.5),
        'w_out': nrm(k[23], (DEPTH, D_MODEL, D_MODEL), D_MODEL ** -0.5),
        'mlp_norm': gain(k[24], (DEPTH, D_MODEL)),
        'w_up': nrm(k[25], (DEPTH, D_MODEL, D_FF), D_MODEL ** -0.5),
        'w_down': nrm(k[26], (DEPTH, D_FF, D_MODEL), D_FF ** -0.5),
        'ple_norm': gain(k[27], (DEPTH, D_MODEL)),
        'w_ple': nrm(k[28], (DEPTH, PLE_DIM, D_MODEL), PLE_DIM ** -0.5),
        'w_ple_gate': nrm(k[29], (DEPTH, D_MODEL, D_MODEL), D_MODEL ** -0.5),
        'final_norm': gain(k[30], (D_MODEL,)),
    }


def reference(x, p, mix_norm, w_in, ssm_conv_w, ssm_conv_b, ssm_dt_bias, ssm_a_log, ssm_d, ssm_norm,
              hgrn_lb_logits, hgrn_norm, mlstm_i_bias, mlstm_f_bias, mlstm_norm,
              lru_conv_w, lru_conv_b, lru_wa, lru_ba, lru_wi, lru_bi, lru_a_param,
              w_branch, w_out, mlp_norm, w_up, w_down, ple_norm, w_ple, w_ple_gate, final_norm):
    lb_all = jnp.cumsum(jax.nn.softmax(hgrn_lb_logits.astype(F32), axis=0), axis=0)
    lb_all = lb_all - lb_all[0]
    split_idx = np.cumsum(IN_WIDTHS)[:-1].tolist()
    for i in range(DEPTH):
        h = rmsnorm(x, mix_norm[i])
        (a_z, a_xbc, a_dt, b_q, b_f, b_i, b_g, c_q, c_k, c_v, c_o, c_i, c_f,
         d_x, d_g, gate_pre) = jnp.split(h @ w_in[i], split_idx, axis=-1)
        y_a = mamba2_mixer(a_z, a_xbc, a_dt, ssm_conv_w[i], ssm_conv_b[i], ssm_dt_bias[i], ssm_a_log[i], ssm_d[i], ssm_norm[i])
        y_b = hgrn2_mixer(b_q, b_f, b_i, b_g, lb_all[i], hgrn_norm[i])
        y_c = mlstm_mixer(c_q, c_k, c_v, c_o, c_i, c_f, mlstm_i_bias[i], mlstm_f_bias[i], mlstm_norm[i])
        y_d = rglru_mixer(d_x, d_g, lru_conv_w[i], lru_conv_b[i], lru_wa[i], lru_ba[i], lru_wi[i], lru_bi[i], lru_a_param[i])
        ys = (y_a, y_b, y_c, y_d)
        merged = jnp.zeros_like(x)
        for br in range(N_BRANCH):
            gate = jax.nn.sigmoid(gate_pre[..., br * D_MODEL:(br + 1) * D_MODEL])
            merged = merged + gate * (ys[br].astype(x.dtype) @ w_branch[i, br])
        x = x + merged @ w_out[i]
        h = rmsnorm(x, mlp_norm[i])
        x = x + jnp.square(jax.nn.relu(h @ w_up[i])) @ w_down[i]
        h = rmsnorm(x, ple_norm[i])
        x = x + (p[i] @ w_ple[i]) * jax.nn.sigmoid(h @ w_ple_gate[i])
    return rmsnorm(x, final_norm)
```

```python
import functools

import jax
import jax.numpy as jnp
from jax import lax
from jax.experimental import pallas as pl
from jax.experimental.pallas import tpu as pltpu

F32 = jnp.float32
BF16 = jnp.bfloat16

D_MODEL = 2048
NORM_EPS = 1e-6
PLE_DIM = 256
N_BRANCH = 4
BRANCH_WIDTH = D_MODEL // 2
CONV_WIDTH = 4
SSM_HEAD_DIM = 64
SSM_HEADS = BRANCH_WIDTH // SSM_HEAD_DIM
SSM_GROUPS = 4
SSM_STATE = 128
SSM_CHUNK = 128
SSM_CONV_DIM = BRANCH_WIDTH + 2 * SSM_GROUPS * SSM_STATE
HGRN_HEAD_DIM = 128
HGRN_HEADS = BRANCH_WIDTH // HGRN_HEAD_DIM
HGRN_CHUNK = 64
HGRN_SUB = 16
MLSTM_HEADS = 4
MLSTM_QK_DIM = BRANCH_WIDTH // (2 * MLSTM_HEADS)
MLSTM_V_DIM = BRANCH_WIDTH // MLSTM_HEADS
MLSTM_CHUNK = 64
LRU_BLOCKS = 8
LRU_BLOCK_DIM = BRANCH_WIDTH // LRU_BLOCKS
LRU_C = 8.0
LRU_CHUNK = 128
D_FF = 4 * D_MODEL

LANES = 128
SUBLANES = 8
VMEM_LIMIT = 48 * 1024 * 1024

COL_XBC = 0
COL_Z = COL_XBC + SSM_CONV_DIM
COL_BQ = COL_Z + BRANCH_WIDTH
COL_BF = COL_BQ + BRANCH_WIDTH
COL_BI = COL_BF + BRANCH_WIDTH
COL_BG = COL_BI + BRANCH_WIDTH
COL_CQK = COL_BG + BRANCH_WIDTH
COL_CV = COL_CQK + BRANCH_WIDTH
COL_CO = COL_CV + BRANCH_WIDTH
COL_DX = COL_CO + BRANCH_WIDTH
COL_DG = COL_DX + BRANCH_WIDTH
COL_GATE = COL_DG + BRANCH_WIDTH
COL_SMALL = COL_GATE + N_BRANCH * D_MODEL
SMALL_WIDTH = 512
P_COLS = COL_SMALL + SMALL_WIDTH
SM_DT = 0
SM_CI = SSM_HEADS
SM_CF = SSM_HEADS + MLSTM_HEADS


def _sigmoid(x):
    return 1.0 / (1.0 + jnp.exp(-x))


def _silu(x):
    return x * _sigmoid(x)


def _softplus(x):
    return jnp.maximum(x, 0.0) + jnp.log1p(jnp.exp(-jnp.abs(x)))


def _log_sigmoid(x):
    return jnp.minimum(x, 0.0) - jnp.log1p(jnp.exp(-jnp.abs(x)))


def _split3(x):
    hi = x.astype(BF16)
    r1 = x - hi.astype(F32)
    mid = r1.astype(BF16)
    lo = (r1 - mid.astype(F32)).astype(BF16)
    return hi, mid, lo


def _dot01_left(a01, x):
    return sum(jnp.dot(a01, p, preferred_element_type=F32) for p in _split3(x))


def _dot01_right(x, a01):
    return sum(jnp.dot(p, a01, preferred_element_type=F32) for p in _split3(x))


def _dot_nt(a, b):
    return lax.dot_general(a, b, (((1,), (1,)), ((), ())), preferred_element_type=F32)


def _tril01(t):
    r = lax.broadcasted_iota(jnp.int32, (t, t), 0)
    c = lax.broadcasted_iota(jnp.int32, (t, t), 1)
    return r >= c


def _causal_conv(x, tail, cw, cb):
    t = x.shape[0]
    row8 = lax.broadcasted_iota(jnp.int32, (SUBLANES, 1), 0)
    acc = cb + cw[CONV_WIDTH - 1:CONV_WIDTH] * x
    for k in range(1, CONV_WIDTH):
        sh = pltpu.roll(x, k, 0)
        tl = pltpu.roll(tail, k, 0)
        first = jnp.where(row8 < k, tl, sh[0:SUBLANES])
        sh = jnp.concatenate([first, sh[SUBLANES:t]], axis=0)
        acc = acc + cw[CONV_WIDTH - 1 - k:CONV_WIDTH - k] * sh
    return acc


def _cparams(sem):
    return pltpu.CompilerParams(dimension_semantics=sem, vmem_limit_bytes=VMEM_LIMIT)


def _rms_to_scratch(x_ref, nw_ref, h_ref):
    x = x_ref[...]
    ms = jnp.mean(x * x, axis=-1, keepdims=True)
    h_ref[...] = (x * lax.rsqrt(ms + NORM_EPS) * nw_ref[...]).astype(BF16)


def _norm_mm_kernel(x_ref, nw_ref, w_ref, o_ref, h_ref, *, relu2):
    @pl.when(pl.program_id(1) == 0)
    def _():
        _rms_to_scratch(x_ref, nw_ref, h_ref)

    acc = jnp.dot(h_ref[...], w_ref[...], preferred_element_type=F32)
    if relu2:
        acc = jnp.square(jnp.maximum(acc, 0.0))
    o_ref[...] = acc.astype(o_ref.dtype)


def _norm_mm(x, nw, w, *, relu2, out_dtype, tm, tn):
    m, k = x.shape
    n = w.shape[1]
    tm = min(tm, m)
    return pl.pallas_call(
        functools.partial(_norm_mm_kernel, relu2=relu2),
        grid=(m // tm, n // tn),
        in_specs=[pl.BlockSpec((tm, k), lambda i, j: (i, 0)),
                  pl.BlockSpec((1, k), lambda i, j: (0, 0)),
                  pl.BlockSpec((k, tn), lambda i, j: (0, j))],
        out_specs=pl.BlockSpec((tm, tn), lambda i, j: (i, j)),
        out_shape=jax.ShapeDtypeStruct((m, n), out_dtype),
        scratch_shapes=[pltpu.VMEM((tm, k), BF16)],
        compiler_params=_cparams(("arbitrary", "arbitrary")),
        name="norm_mm_relu2" if relu2 else "norm_mm",
    )(x, nw, w)


def _ple_kernel(x_ref, nw_ref, wg_ref, p_ref, wp_ref, xc_ref, o_ref, h_ref):
    @pl.when(pl.program_id(1) == 0)
    def _():
        _rms_to_scratch(x_ref, nw_ref, h_ref)

    gate = jnp.dot(h_ref[...], wg_ref[...], preferred_element_type=F32)
    emb = jnp.dot(p_ref[...].astype(BF16), wp_ref[...], preferred_element_type=F32)
    o_ref[...] = xc_ref[...] + emb * _sigmoid(gate)


def _ple(x, nw, wg, p, wp, *, tm, tn):
    m, k = x.shape
    tm = min(tm, m)
    kp = p.shape[1]
    return pl.pallas_call(
        _ple_kernel,
        grid=(m // tm, k // tn),
        in_specs=[pl.BlockSpec((tm, k), lambda i, j: (i, 0)),
                  pl.BlockSpec((1, k), lambda i, j: (0, 0)),
                  pl.BlockSpec((k, tn), lambda i, j: (0, j)),
                  pl.BlockSpec((tm, kp), lambda i, j: (i, 0)),
                  pl.BlockSpec((kp, tn), lambda i, j: (0, j)),
                  pl.BlockSpec((tm, tn), lambda i, j: (i, j))],
        out_specs=pl.BlockSpec((tm, tn), lambda i, j: (i, j)),
        out_shape=jax.ShapeDtypeStruct((m, k), F32),
        scratch_shapes=[pltpu.VMEM((tm, k), BF16)],
        compiler_params=_cparams(("arbitrary", "arbitrary")),
        name="ple",
    )(x, nw, wg, p, wp, x)


def _mm_res_kernel(a_ref, w_ref, r_ref, o_ref, acc_ref):
    kk = pl.program_id(2)

    @pl.when(kk == 0)
    def _():
        acc_ref[...] = r_ref[...]

    acc_ref[...] += jnp.dot(a_ref[...], w_ref[...], preferred_element_type=F32)

    @pl.when(kk == pl.num_programs(2) - 1)
    def _():
        o_ref[...] = acc_ref[...]


def _mm_res(a, w, r, *, tm, tn, tk):
    m, k = a.shape
    n = w.shape[1]
    tm = min(tm, m)
    return pl.pallas_call(
        _mm_res_kernel,
        grid=(m // tm, n // tn, k // tk),
        in_specs=[pl.BlockSpec((tm, tk), lambda i, j, q: (i, q)),
                  pl.BlockSpec((tk, tn), lambda i, j, q: (q, j)),
                  pl.BlockSpec((tm, tn), lambda i, j, q: (i, j))],
        out_specs=pl.BlockSpec((tm, tn), lambda i, j, q: (i, j)),
        out_shape=jax.ShapeDtypeStruct((m, n), F32),
        scratch_shapes=[pltpu.VMEM((tm, tn), F32)],
        compiler_params=_cparams(("arbitrary", "arbitrary", "arbitrary")),
        name="mm_res",
    )(a, w, r)


def _merge_kernel(ya_ref, yb_ref, yc_ref, yd_ref, ga_ref, gb_ref, gc_ref, gd_ref, wb_ref, o_ref):
    acc = None
    for br, (y_ref, g_ref) in enumerate(((ya_ref, ga_ref), (yb_ref, gb_ref), (yc_ref, gc_ref), (yd_ref, gd_ref))):
        t = _sigmoid(g_ref[...]) * jnp.dot(y_ref[...], wb_ref[br], preferred_element_type=F32)
        acc = t if acc is None else acc + t
    o_ref[...] = acc.astype(o_ref.dtype)


def _merge(ys, proj2d, wb, *, tm, tn):
    m = proj2d.shape[0]
    tm = min(tm, m)
    y_spec = pl.BlockSpec((tm, BRANCH_WIDTH), lambda i, j: (i, 0))
    gate_specs = [pl.BlockSpec((tm, tn), functools.partial(
        lambda i, j, off: (i, off + j), off=(COL_GATE + br * D_MODEL) // tn)) for br in range(N_BRANCH)]
    return pl.pallas_call(
        _merge_kernel,
        grid=(m // tm, D_MODEL // tn),
        in_specs=[y_spec] * N_BRANCH + gate_specs + [pl.BlockSpec((N_BRANCH, BRANCH_WIDTH, tn), lambda i, j: (0, 0, j))],
        out_specs=pl.BlockSpec((tm, tn), lambda i, j: (i, j)),
        out_shape=jax.ShapeDtypeStruct((m, D_MODEL), BF16),
        compiler_params=_cparams(("arbitrary", "arbitrary")),
        name="merge",
    )(*ys, proj2d, proj2d, proj2d, proj2d, wb)


def _final_norm_kernel(x_ref, nw_ref, o_ref):
    x = x_ref[...]
    ms = jnp.mean(x * x, axis=-1, keepdims=True)
    o_ref[...] = x * lax.rsqrt(ms + NORM_EPS) * nw_ref[...]


def _final_norm(x, nw, *, tm):
    m, k = x.shape
    tm = min(tm, m)
    return pl.pallas_call(
        _final_norm_kernel,
        grid=(m // tm,),
        in_specs=[pl.BlockSpec((tm, k), lambda i: (i, 0)), pl.BlockSpec((1, k), lambda i: (0, 0))],
        out_specs=pl.BlockSpec((tm, k), lambda i: (i, 0)),
        out_shape=jax.ShapeDtypeStruct((m, k), F32),
        compiler_params=_cparams(("arbitrary",)),
        name="final_norm",
    )(x, nw)


def _ssd_kernel(xbc_ref, z_ref, sm_ref, cw_ref, cb_ref, dtb_ref, alog_ref, dsk_ref, nw_ref, e_ref,
                o_ref, tail_ref, state_ref):
    t = SSM_CHUNK
    gw = BRANCH_WIDTH // SSM_GROUPS

    @pl.when(pl.program_id(1) == 0)
    def _():
        tail_ref[...] = jnp.zeros_like(tail_ref)
        state_ref[...] = jnp.zeros_like(state_ref)

    xbc = xbc_ref[0]
    u = _silu(_causal_conv(xbc, tail_ref[...], cw_ref[...], cb_ref[...]))
    tail_ref[...] = xbc[t - SUBLANES:t]
    xs = u[:, :BRANCH_WIDTH]
    bm = u[:, BRANCH_WIDTH:BRANCH_WIDTH + SSM_GROUPS * SSM_STATE]
    cm = u[:, BRANCH_WIDTH + SSM_GROUPS * SSM_STATE:]

    causal = _tril01(t)
    dt = _softplus(sm_ref[0] + dtb_ref[...])
    adt = dt * (-jnp.exp(alog_ref[...]))
    cs = _dot01_left(causal.astype(BF16), adt)
    cs_t = cs.T
    dt_t = dt.T
    tot = cs[t - 1:t]
    e01 = e_ref[...]
    w_exp = _dot01_right(jnp.exp(tot - cs) * dt, e01)
    ecs_exp = _dot01_right(jnp.exp(cs), e01)
    etot_exp = _dot01_right(jnp.broadcast_to(jnp.exp(tot), (SUBLANES, LANES)), e01)[0:1]

    z = z_ref[0]
    lane_hi = lax.broadcasted_iota(jnp.int32, (1, LANES), 1) >= SSM_HEAD_DIM
    heads_per_group = SSM_HEADS // SSM_GROUPS
    for g in range(SSM_GROUPS):
        bg = bm[:, g * SSM_STATE:(g + 1) * SSM_STATE]
        cg = cm[:, g * SSM_STATE:(g + 1) * SSM_STATE]
        bg16 = bg.astype(BF16)
        cg16 = cg.astype(BF16)
        cb_mat = _dot_nt(cg16, bg16)
        st = state_ref[g]
        xg = xs[:, g * gw:(g + 1) * gw]
        y = jnp.dot(cg16, st.astype(BF16), preferred_element_type=F32) * ecs_exp[:, g * gw:(g + 1) * gw]
        pairs = []
        for pr in range(gw // LANES):
            xpair = xg[:, pr * LANES:(pr + 1) * LANES]
            ypair = None
            for hh in range(LANES // SSM_HEAD_DIM):
                h = g * heads_per_group + pr * (LANES // SSM_HEAD_DIM) + hh
                seg = jnp.minimum(cs[:, h:h + 1] - cs_t[h:h + 1, :], 0.0)
                lmat = jnp.where(causal, jnp.exp(seg), 0.0)
                mh = (cb_mat * lmat * dt_t[h:h + 1, :]).astype(BF16)
                xm = jnp.where(lane_hi if hh else jnp.logical_not(lane_hi), xpair, 0.0).astype(BF16)
                part = jnp.dot(mh, xm, preferred_element_type=F32)
                ypair = part if ypair is None else ypair + part
            pairs.append(ypair)
        y = y + jnp.concatenate(pairs, axis=1) + dsk_ref[:, g * gw:(g + 1) * gw] * xg
        xw = (xg * w_exp[:, g * gw:(g + 1) * gw]).astype(BF16)
        state_ref[g] = st * etot_exp[:, g * gw:(g + 1) * gw] + jnp.dot(bg.T.astype(BF16), xw, preferred_element_type=F32)
        y = y * _silu(z[:, g * gw:(g + 1) * gw])
        ms = jnp.mean(y * y, axis=-1, keepdims=True)
        o_ref[0, :, g * gw:(g + 1) * gw] = (y * lax.rsqrt(ms + NORM_EPS) * nw_ref[:, g * gw:(g + 1) * gw]).astype(o_ref.dtype)


def _ssd(proj, cw, cb, dtb, alog, dsk, nw, e01):
    b, l, _ = proj.shape
    t = SSM_CHUNK
    const = lambda shape: pl.BlockSpec(shape, lambda i, c: (0,) * len(shape))
    return pl.pallas_call(
        _ssd_kernel,
        grid=(b, l // t),
        in_specs=[pl.BlockSpec((1, t, SSM_CONV_DIM), lambda i, c: (i, c, COL_XBC // SSM_CONV_DIM)),
                  pl.BlockSpec((1, t, BRANCH_WIDTH), lambda i, c: (i, c, COL_Z // BRANCH_WIDTH)),
                  pl.BlockSpec((1, t, LANES), lambda i, c: (i, c, COL_SMALL // LANES)),
                  const((CONV_WIDTH, SSM_CONV_DIM)), const((1, SSM_CONV_DIM)),
                  const((1, LANES)), const((1, LANES)),
                  const((1, BRANCH_WIDTH)), const((1, BRANCH_WIDTH)), const((LANES, BRANCH_WIDTH))],
        out_specs=pl.BlockSpec((1, t, BRANCH_WIDTH), lambda i, c: (i, c, 0)),
        out_shape=jax.ShapeDtypeStruct((b, l, BRANCH_WIDTH), BF16),
        scratch_shapes=[pltpu.VMEM((SUBLANES, SSM_CONV_DIM), F32),
                        pltpu.VMEM((SSM_GROUPS, SSM_STATE, BRANCH_WIDTH // SSM_GROUPS), F32)],
        compiler_params=_cparams(("arbitrary", "arbitrary")),
        name="ssd",
    )(proj, proj, proj, cw, cb, dtb, alog, dsk, nw, e01)


def _hgrn_kernel(q_ref, f_ref, i_ref, g_ref, lbl_ref, nw_ref, o_ref, st_ref, k_s, gc_s, *, layer):
    t = HGRN_CHUNK
    sub = HGRN_SUB
    hd = HGRN_HEAD_DIM

    @pl.when(pl.program_id(1) == 0)
    def _():
        st_ref[...] = jnp.zeros_like(st_ref)

    logits = lbl_ref[...]
    ex = jnp.exp(logits - jnp.max(logits, axis=0, keepdims=True))
    sm = ex / jnp.sum(ex, axis=0, keepdims=True)
    lb = jnp.zeros((1, BRANCH_WIDTH), F32)
    for r in range(1, layer + 1):
        lb = lb + sm[r:r + 1]

    fp = f_ref[0]
    la = jnp.log(lb)
    lbb = jnp.log1p(-lb) + _log_sigmoid(fp)
    logf = jnp.maximum(la, lbb) + jnp.log1p(jnp.exp(-jnp.abs(la - lbb)))
    k_s[...] = (1.0 - lb) * _sigmoid(-fp)
    gc_s[...] = _dot01_left(_tril01(t).astype(BF16), logf)
    qq = _silu(q_ref[0]) * (hd ** -0.5)
    vv = i_ref[0]
    gg = g_ref[0]

    ones16 = jnp.ones((hd, hd), BF16)
    lane_t = lax.broadcasted_iota(jnp.int32, (sub, t), 1)
    row_s = lax.broadcasted_iota(jnp.int32, (sub, t), 0)
    for h in range(HGRN_HEADS):
        sl = slice(h * hd, (h + 1) * hd)
        q = qq[:, sl]
        v = vv[:, sl]
        k = k_s[:, sl]
        gc = gc_s[:, sl]
        glast = gc_s[t - 1:t, sl]
        st = st_ref[h]
        o = _dot_nt((q * jnp.exp(gc)).astype(BF16), st.astype(BF16))
        a_rows = []
        for sc in range(t // sub):
            r0 = sc * sub
            q_i = q[r0:r0 + sub]
            g_i = gc[r0:r0 + sub]
            prods = []
            for j in range(sub):
                k_j = k_s[r0 + j:r0 + j + 1, sl]
                g_j = gc_s[r0 + j:r0 + j + 1, sl]
                prods.append(q_i * k_j * jnp.exp(jnp.minimum(g_i - g_j, 0.0)))
            rsum = jnp.dot(jnp.concatenate(prods, axis=0).astype(BF16), ones16, preferred_element_type=F32)
            blk = jnp.zeros((sub, t), F32)
            for j in range(sub):
                sel = (lane_t == r0 + j) & (row_s >= j)
                blk = jnp.where(sel, rsum[j * sub:(j + 1) * sub, :t], blk)
            if sc > 0:
                g_r = gc_s[r0:r0 + 1, sl]
                qs = (q_i * jnp.exp(g_i - g_r)).astype(BF16)
                ks = (k * jnp.exp(jnp.minimum(g_r - gc, 0.0))).astype(BF16)
                blk = jnp.where(lane_t < r0, _dot_nt(qs, ks), blk)
            a_rows.append(blk)
        attn = jnp.concatenate(a_rows, axis=0)
        o = o + jnp.dot(attn.astype(BF16), v.astype(BF16), preferred_element_type=F32)
        kd = (k * jnp.exp(glast - gc)).astype(BF16)
        st_ref[h] = st * jnp.exp(glast) + jnp.dot(v.T.astype(BF16), kd, preferred_element_type=F32)
        ms = jnp.mean(o * o, axis=-1, keepdims=True)
        o = o * lax.rsqrt(ms + NORM_EPS) * nw_ref[:, sl]
        o_ref[0, :, sl] = (o * _silu(gg[:, sl])).astype(o_ref.dtype)


def _hgrn(proj, lb_logits, nw, layer):
    b, l, _ = proj.shape
    t = HGRN_CHUNK
    w = BRANCH_WIDTH
    seg = lambda col: pl.BlockSpec((1, t, w), functools.partial(lambda i, c, cb: (i, c, cb), cb=col // w))
    return pl.pallas_call(
        functools.partial(_hgrn_kernel, layer=layer),
        grid=(b, l // t),
        in_specs=[seg(COL_BQ), seg(COL_BF), seg(COL_BI), seg(COL_BG),
                  pl.BlockSpec(lb_logits.shape, lambda i, c: (0, 0)),
                  pl.BlockSpec((1, w), lambda i, c: (0, 0))],
        out_specs=pl.BlockSpec((1, t, w), lambda i, c: (i, c, 0)),
        out_shape=jax.ShapeDtypeStruct((b, l, w), BF16),
        scratch_shapes=[pltpu.VMEM((HGRN_HEADS, HGRN_HEAD_DIM, HGRN_HEAD_DIM), F32),
                        pltpu.VMEM((t, w), F32), pltpu.VMEM((t, w), F32)],
        compiler_params=_cparams(("arbitrary", "arbitrary")),
        name="hgrn2",
    )(proj, proj, proj, proj, lb_logits, nw)


def _mlstm_kernel(qk_ref, v_ref, o_ref_in, sm_ref, gb_ref, nw_ref, out_ref, c_ref, m_ref):
    t = MLSTM_CHUNK
    dk = MLSTM_QK_DIM
    dv = MLSTM_V_DIM
    dve = dv + LANES

    @pl.when(pl.program_id(1) == 0)
    def _():
        c_ref[...] = jnp.zeros_like(c_ref)
        m_ref[...] = jnp.zeros_like(m_ref)

    causal = _tril01(t)
    gates = sm_ref[0] + gb_ref[...]
    lf = _log_sigmoid(gates)
    bcum = _dot01_left(causal.astype(BF16), lf)
    eye = (lax.broadcasted_iota(jnp.int32, (LANES, LANES), 0)
           == lax.broadcasted_iota(jnp.int32, (LANES, LANES), 1)).astype(BF16)
    bcum_t = sum(_dot_nt(eye, p) for p in _split3(bcum))
    gates_t = sum(_dot_nt(eye, p) for p in _split3(gates))
    lane_e = lax.broadcasted_iota(jnp.int32, (1, LANES), 1)
    ones_col = jnp.where(lane_e == 0, 1.0, 0.0)

    qk = qk_ref[0]
    vv = v_ref[0]
    og = o_ref_in[0]
    for h in range(MLSTM_HEADS):
        q = qk[:, h * dk:(h + 1) * dk] * (dk ** -0.5)
        k = qk[:, MLSTM_HEADS * dk + h * dk:MLSTM_HEADS * dk + (h + 1) * dk]
        v = vv[:, h * dv:(h + 1) * dv]
        v_ext = jnp.concatenate([v, jnp.broadcast_to(ones_col, (t, LANES))], axis=1).astype(BF16)
        bc = bcum[:, SM_CF + h:SM_CF + h + 1]
        ig = gates[:, SM_CI + h:SM_CI + h + 1]
        bc_r = bcum_t[SM_CF + h:SM_CF + h + 1, :]
        ig_r = gates_t[SM_CI + h:SM_CI + h + 1, :]
        m_st = m_ref[h][:, 0:1]
        c_st = c_ref[h]

        dlog = jnp.where(causal, bc - bc_r + ig_r, -jnp.inf)
        inter_log = bc + m_st
        m = jnp.maximum(inter_log, jnp.max(dlog, axis=-1, keepdims=True))
        w_intra = jnp.exp(dlog - m)
        w_inter = jnp.exp(inter_log - m)
        q16 = q.astype(BF16)
        s = _dot_nt(q16, k.astype(BF16)) * w_intra
        nd = jnp.dot(s.astype(BF16), v_ext, preferred_element_type=F32) \
            + w_inter * jnp.dot(q16, c_st.astype(BF16), preferred_element_type=F32)
        num = nd[:, :dv]
        den = nd[:, dv:dv + 1]
        hh = num / jnp.maximum(jnp.abs(den), jnp.exp(-m))

        b_last = bcum[t - 1:t, SM_CF + h:SM_CF + h + 1]
        log_w = b_last - bc + ig
        m_new = jnp.maximum(b_last + m_st, jnp.max(log_w, axis=0, keepdims=True))
        wk = jnp.exp(log_w - m_new)
        decay = jnp.exp(b_last + m_st - m_new)
        kv = jnp.dot(k.T.astype(BF16), (wk * v_ext.astype(F32)).astype(BF16), preferred_element_type=F32)
        c_ref[h] = decay * c_st + kv
        m_ref[h] = jnp.broadcast_to(m_new, (1, LANES))

        ms = jnp.mean(hh * hh, axis=-1, keepdims=True)
        hn = hh * lax.rsqrt(ms + NORM_EPS) * nw_ref[:, h * dv:(h + 1) * dv]
        out_ref[0, :, h * dv:(h + 1) * dv] = (hn * _sigmoid(og[:, h * dv:(h + 1) * dv])).astype(out_ref.dtype)


def _mlstm(proj, gate_bias, nw):
    b, l, _ = proj.shape
    t = MLSTM_CHUNK
    w = BRANCH_WIDTH
    seg = lambda col: pl.BlockSpec((1, t, w), functools.partial(lambda i, c, cb: (i, c, cb), cb=col // w))
    return pl.pallas_call(
        _mlstm_kernel,
        grid=(b, l // t),
        in_specs=[seg(COL_CQK), seg(COL_CV), seg(COL_CO),
                  pl.BlockSpec((1, t, LANES), lambda i, c: (i, c, COL_SMALL // LANES)),
                  pl.BlockSpec((1, LANES), lambda i, c: (0, 0)),
                  pl.BlockSpec((1, w), lambda i, c: (0, 0))],
        out_specs=pl.BlockSpec((1, t, w), lambda i, c: (i, c, 0)),
        out_shape=jax.ShapeDtypeStruct((b, l, w), BF16),
        scratch_shapes=[pltpu.VMEM((MLSTM_HEADS, MLSTM_QK_DIM, MLSTM_V_DIM + LANES), F32),
                        pltpu.VMEM((MLSTM_HEADS, 1, LANES), F32)],
        compiler_params=_cparams(("arbitrary", "arbitrary")),
        name="mlstm",
    )(proj, proj, proj, proj, gate_bias, nw)


def _lru_kernel(x_ref, g_ref, cw_ref, cb_ref, wa_ref, ba_ref, wi_ref, bi_ref, ap_ref, o_ref, tail_ref, h_ref):
    t = LRU_CHUNK
    bd = LRU_BLOCK_DIM

    @pl.when(pl.program_id(1) == 0)
    def _():
        tail_ref[...] = jnp.zeros_like(tail_ref)
        h_ref[...] = jnp.zeros_like(h_ref)

    x = x_ref[0]
    xc = _causal_conv(x, tail_ref[...], cw_ref[...], cb_ref[...])
    tail_ref[...] = x[t - SUBLANES:t]
    xc16 = xc.astype(BF16)
    r_parts, i_parts = [], []
    for n in range(LRU_BLOCKS):
        xb = xc16[:, n * bd:(n + 1) * bd]
        r_parts.append(jnp.dot(xb, wa_ref[n], preferred_element_type=F32))
        i_parts.append(jnp.dot(xb, wi_ref[n], preferred_element_type=F32))
    r = _sigmoid(jnp.concatenate(r_parts, axis=1) + ba_ref[...])
    ig = _sigmoid(jnp.concatenate(i_parts, axis=1) + bi_ref[...])
    log_a = -LRU_C * r * _softplus(-ap_ref[...])
    a = jnp.exp(log_a)
    u = xc * ig * jnp.sqrt(-jnp.tanh(log_a) * (a * a + 1.0))

    row = lax.broadcasted_iota(jnp.int32, (t, 1), 0)
    d = 1
    while d < t:
        keep = row >= d
        u = jnp.where(keep, a * pltpu.roll(u, d, 0) + u, u)
        a = jnp.where(keep, a * pltpu.roll(a, d, 0), a)
        d *= 2
    hseq = u + a * h_ref[0:1]
    h_ref[...] = jnp.broadcast_to(hseq[t - 1:t], h_ref.shape)
    o_ref[0] = (hseq * jax.nn.gelu(g_ref[0], approximate=True)).astype(o_ref.dtype)


def _lru(proj, cw, cb, wa, ba, wi, bi, ap):
    b, l, _ = proj.shape
    t = LRU_CHUNK
    w = BRANCH_WIDTH
    seg = lambda col: pl.BlockSpec((1, t, w), functools.partial(lambda i, c, cb_: (i, c, cb_), cb_=col // w))
    row = pl.BlockSpec((1, w), lambda i, c: (0, 0))
    blk = pl.BlockSpec((LRU_BLOCKS, LRU_BLOCK_DIM, LRU_BLOCK_DIM), lambda i, c: (0, 0, 0))
    return pl.pallas_call(
        _lru_kernel,
        grid=(b, l // t),
        in_specs=[seg(COL_DX), seg(COL_DG), pl.BlockSpec((CONV_WIDTH, w), lambda i, c: (0, 0)), row,
                  blk, row, blk, row, row],
        out_specs=pl.BlockSpec((1, t, w), lambda i, c: (i, c, 0)),
        out_shape=jax.ShapeDtypeStruct((b, l, w), BF16),
        scratch_shapes=[pltpu.VMEM((SUBLANES, w), F32), pltpu.VMEM((SUBLANES, w), F32)],
        compiler_params=_cparams(("arbitrary", "arbitrary")),
        name="rglru",
    )(proj, proj, cw, cb, wa, ba, wi, bi, ap)


def _regroup_w_in(w):
    widths = (BRANCH_WIDTH, SSM_CONV_DIM, SSM_HEADS, BRANCH_WIDTH, BRANCH_WIDTH, BRANCH_WIDTH, BRANCH_WIDTH,
              MLSTM_HEADS * MLSTM_QK_DIM, MLSTM_HEADS * MLSTM_QK_DIM, BRANCH_WIDTH, BRANCH_WIDTH,
              MLSTM_HEADS, MLSTM_HEADS, BRANCH_WIDTH, BRANCH_WIDTH, N_BRANCH * D_MODEL)
    offs = [0]
    for wd in widths:
        offs.append(offs[-1] + wd)
    seg = [w[:, offs[i]:offs[i + 1]] for i in range(len(widths))]
    (a_z, a_xbc, a_dt, b_q, b_f, b_i, b_g, c_q, c_k, c_v, c_o, c_i, c_f, d_x, d_g, gate) = seg
    pad = jnp.zeros((w.shape[0], SMALL_WIDTH - SSM_HEADS - 2 * MLSTM_HEADS), w.dtype)
    out = jnp.concatenate([a_xbc, a_z, b_q, b_f, b_i, b_g, c_q, c_k, c_v, c_o, d_x, d_g, gate,
                           a_dt, c_i, c_f, pad], axis=1)
    return out.astype(BF16)


def _pad_lanes(v, start):
    out = jnp.zeros((1, LANES), F32)
    return out.at[0, start:start + v.shape[0]].set(v.astype(F32))


def kernel(x, p, mix_norm, w_in, ssm_conv_w, ssm_conv_b, ssm_dt_bias, ssm_a_log, ssm_d, ssm_norm, hgrn_lb_logits, hgrn_norm, mlstm_i_bias, mlstm_f_bias, mlstm_norm, lru_conv_w, lru_conv_b, lru_wa, lru_ba, lru_wi, lru_bi, lru_a_param, w_branch, w_out, mlp_norm, w_up, w_down, ple_norm, w_ple, w_ple_gate, final_norm):
    b, l, d = x.shape
    depth = w_in.shape[0]
    n = b * l
    row = lambda v: v.astype(F32).reshape(1, -1)
    e01 = (jnp.arange(LANES)[:, None] == (jnp.arange(BRANCH_WIDTH)[None, :] // SSM_HEAD_DIM)).astype(BF16)

    xf = x.reshape(n, d)
    for i in range(depth):
        proj = _norm_mm(xf, row(mix_norm[i]), _regroup_w_in(w_in[i]), relu2=False, out_dtype=F32, tm=1024, tn=512)
        proj3 = proj.reshape(b, l, P_COLS)

        y_a = _ssd(proj3, ssm_conv_w[i].astype(F32), row(ssm_conv_b[i]), _pad_lanes(ssm_dt_bias[i], SM_DT),
                   _pad_lanes(ssm_a_log[i], SM_DT), row(jnp.repeat(ssm_d[i], SSM_HEAD_DIM)), row(ssm_norm[i]), e01)
        y_b = _hgrn(proj3, hgrn_lb_logits.astype(F32), row(hgrn_norm[i]), i)
        gate_bias = _pad_lanes(jnp.concatenate([mlstm_i_bias[i], mlstm_f_bias[i]]), SM_CI)
        y_c = _mlstm(proj3, gate_bias, row(mlstm_norm[i]))
        y_d = _lru(proj3, lru_conv_w[i].astype(F32), row(lru_conv_b[i]), lru_wa[i].astype(BF16), row(lru_ba[i]),
                   lru_wi[i].astype(BF16), row(lru_bi[i]), row(lru_a_param[i]))

        ys = [y.reshape(n, BRANCH_WIDTH) for y in (y_a, y_b, y_c, y_d)]
        merged = _merge(ys, proj, w_branch[i].astype(BF16), tm=512, tn=512)
        xf = _mm_res(merged, w_out[i].astype(BF16), xf, tm=1024, tn=1024, tk=D_MODEL)

        up = _norm_mm(xf, row(mlp_norm[i]), w_up[i].astype(BF16), relu2=True, out_dtype=BF16, tm=1024, tn=512)
        xf = _mm_res(up, w_down[i].astype(BF16), xf, tm=1024, tn=1024, tk=2048)

        xf = _ple(xf, row(ple_norm[i]), w_ple_gate[i].astype(BF16), p[i].reshape(n, PLE_DIM), w_ple[i].astype(BF16),
                  tm=1024, tn=512)
    return _final_norm(xf, row(final_norm), tm=512).reshape(b, l, d)
```

```python
import functools
import math

import jax
import jax.numpy as jnp
from jax import lax
from jax.experimental import pallas as pl
from jax.experimental.pallas import tpu as pltpu

F32 = jnp.float32
BF16 = jnp.bfloat16

D_MODEL = 2048
NORM_EPS = 1e-6
PLE_DIM = 256
N_BRANCH = 4
BRANCH_WIDTH = D_MODEL // 2
CONV_WIDTH = 4
SSM_HEAD_DIM = 64
SSM_HEADS = BRANCH_WIDTH // SSM_HEAD_DIM
SSM_GROUPS = 4
SSM_STATE = 128
SSM_CHUNK = 128
SSM_CONV_DIM = BRANCH_WIDTH + 2 * SSM_GROUPS * SSM_STATE
HGRN_HEAD_DIM = 128
HGRN_HEADS = BRANCH_WIDTH // HGRN_HEAD_DIM
HGRN_CHUNK = 64
HGRN_SUB = 16
MLSTM_HEADS = 4
MLSTM_QK_DIM = BRANCH_WIDTH // (2 * MLSTM_HEADS)
MLSTM_V_DIM = BRANCH_WIDTH // MLSTM_HEADS
MLSTM_CHUNK = 64
LRU_BLOCKS = 8
LRU_BLOCK_DIM = BRANCH_WIDTH // LRU_BLOCKS
LRU_C = 8.0
LRU_CHUNK = 128
D_FF = 4 * D_MODEL

LANES = 128
SUBLANES = 8
VMEM_LIMIT = 48 * 1024 * 1024
LOG2E = math.log2(math.e)
SEQS_PER_STEP = 2

COL_XBC = 0
COL_Z = COL_XBC + SSM_CONV_DIM
COL_BQ = COL_Z + BRANCH_WIDTH
COL_BF = COL_BQ + BRANCH_WIDTH
COL_BI = COL_BF + BRANCH_WIDTH
COL_BG = COL_BI + BRANCH_WIDTH
COL_CQK = COL_BG + BRANCH_WIDTH
COL_CV = COL_CQK + BRANCH_WIDTH
COL_CO = COL_CV + BRANCH_WIDTH
COL_DX = COL_CO + BRANCH_WIDTH
COL_DG = COL_DX + BRANCH_WIDTH
COL_GATE = COL_DG + BRANCH_WIDTH
P_COLS = COL_GATE + N_BRANCH * D_MODEL
SM_DT = 0
SM_CI = SSM_HEADS
SM_CF = SSM_HEADS + MLSTM_HEADS


def _sigmoid(x):
    return 1.0 / (1.0 + jnp.exp(-x))


def _silu(x):
    return x * _sigmoid(x)


def _softplus(x):
    return jnp.maximum(x, 0.0) + jnp.log1p(jnp.exp(-jnp.abs(x)))


def _log_sigmoid(x):
    return jnp.minimum(x, 0.0) - jnp.log1p(jnp.exp(-jnp.abs(x)))


def _split3(x):
    hi = x.astype(BF16)
    r1 = x - hi.astype(F32)
    mid = r1.astype(BF16)
    lo = (r1 - mid.astype(F32)).astype(BF16)
    return hi, mid, lo


def _dot01_left(a01, x):
    return sum(jnp.dot(a01, p, preferred_element_type=F32) for p in _split3(x))


def _dot01_right(x, a01):
    return sum(jnp.dot(p, a01, preferred_element_type=F32) for p in _split3(x))


def _dot_nt(a, b):
    return lax.dot_general(a, b, (((1,), (1,)), ((), ())), preferred_element_type=F32)


def _tril01(t):
    r = lax.broadcasted_iota(jnp.int32, (t, t), 0)
    c = lax.broadcasted_iota(jnp.int32, (t, t), 1)
    return r >= c


def _causal_conv(x, ext_ref, s, cw, cb):
    t = x.shape[0]
    ext_ref[s, SUBLANES:SUBLANES + t, :] = x
    acc = cb + cw[CONV_WIDTH - 1:CONV_WIDTH] * x
    for k in range(1, CONV_WIDTH):
        acc = acc + cw[CONV_WIDTH - 1 - k:CONV_WIDTH - k] * ext_ref[s, SUBLANES - k:SUBLANES - k + t, :]
    ext_ref[s, 0:SUBLANES, :] = x[t - SUBLANES:t]
    return acc


def _cparams(sem):
    return pltpu.CompilerParams(dimension_semantics=sem, vmem_limit_bytes=VMEM_LIMIT)


def _rms_to_scratch(x_ref, nw_ref, h_ref):
    x = x_ref[...]
    ms = jnp.mean(x * x, axis=-1, keepdims=True)
    h_ref[...] = (x * lax.rsqrt(ms + NORM_EPS) * nw_ref[...]).astype(BF16)


def _inproj_kernel(x_ref, nw_ref, w_ref, ws_ref, o_ref, os_ref, h_ref):
    @pl.when(pl.program_id(1) == 0)
    def _():
        _rms_to_scratch(x_ref, nw_ref, h_ref)
        os_ref[...] = jnp.dot(h_ref[...], ws_ref[...], preferred_element_type=F32)

    o_ref[...] = jnp.dot(h_ref[...], w_ref[...], preferred_element_type=F32)


def _inproj(x, nw, w, ws, *, tm, tn):
    m, k = x.shape
    n = w.shape[1]
    tm = min(tm, m)
    return pl.pallas_call(
        _inproj_kernel,
        grid=(m // tm, n // tn),
        in_specs=[pl.BlockSpec((tm, k), lambda i, j: (i, 0)),
                  pl.BlockSpec((1, k), lambda i, j: (0, 0)),
                  pl.BlockSpec((k, tn), lambda i, j: (0, j)),
                  pl.BlockSpec((k, LANES), lambda i, j: (0, 0))],
        out_specs=[pl.BlockSpec((tm, tn), lambda i, j: (i, j)),
                   pl.BlockSpec((tm, LANES), lambda i, j: (i, 0))],
        out_shape=[jax.ShapeDtypeStruct((m, n), F32), jax.ShapeDtypeStruct((m, LANES), F32)],
        scratch_shapes=[pltpu.VMEM((tm, k), BF16)],
        compiler_params=_cparams(("arbitrary", "arbitrary")),
        name="inproj",
    )(x, nw, w, ws)


def _up_kernel(x_ref, nw_ref, w_ref, o_ref, h_ref):
    @pl.when(pl.program_id(1) == 0)
    def _():
        _rms_to_scratch(x_ref, nw_ref, h_ref)

    acc = jnp.dot(h_ref[...], w_ref[...], preferred_element_type=F32)
    o_ref[...] = jnp.square(jnp.maximum(acc, 0.0)).astype(o_ref.dtype)


def _up(x, nw, w, *, tm, tn):
    m, k = x.shape
    n = w.shape[1]
    tm = min(tm, m)
    return pl.pallas_call(
        _up_kernel,
        grid=(m // tm, n // tn),
        in_specs=[pl.BlockSpec((tm, k), lambda i, j: (i, 0)),
                  pl.BlockSpec((1, k), lambda i, j: (0, 0)),
                  pl.BlockSpec((k, tn), lambda i, j: (0, j))],
        out_specs=pl.BlockSpec((tm, tn), lambda i, j: (i, j)),
        out_shape=jax.ShapeDtypeStruct((m, n), BF16),
        scratch_shapes=[pltpu.VMEM((tm, k), BF16)],
        compiler_params=_cparams(("arbitrary", "arbitrary")),
        name="mlp_up",
    )(x, nw, w)


def _ple_kernel(x_ref, nw_ref, wg_ref, p_ref, wp_ref, o_ref, h_ref, *, tn):
    j = pl.program_id(1)

    @pl.when(j == 0)
    def _():
        _rms_to_scratch(x_ref, nw_ref, h_ref)

    gate = jnp.dot(h_ref[...], wg_ref[...], preferred_element_type=F32)
    emb = jnp.dot(p_ref[...].astype(BF16), wp_ref[...], preferred_element_type=F32)
    xc = x_ref[:, pl.ds(pl.multiple_of(j * tn, tn), tn)]
    o_ref[...] = xc + emb * _sigmoid(gate)


def _ple(x, nw, wg, p, wp, *, tm, tn):
    m, k = x.shape
    tm = min(tm, m)
    kp = p.shape[1]
    return pl.pallas_call(
        functools.partial(_ple_kernel, tn=tn),
        grid=(m // tm, k // tn),
        in_specs=[pl.BlockSpec((tm, k), lambda i, j: (i, 0)),
                  pl.BlockSpec((1, k), lambda i, j: (0, 0)),
                  pl.BlockSpec((k, tn), lambda i, j: (0, j)),
                  pl.BlockSpec((tm, kp), lambda i, j: (i, 0)),
                  pl.BlockSpec((kp, tn), lambda i, j: (0, j))],
        out_specs=pl.BlockSpec((tm, tn), lambda i, j: (i, j)),
        out_shape=jax.ShapeDtypeStruct((m, k), F32),
        scratch_shapes=[pltpu.VMEM((tm, k), BF16)],
        compiler_params=_cparams(("arbitrary", "arbitrary")),
        name="ple",
    )(x, nw, wg, p, wp)


def _mm_res_kernel(a_ref, w_ref, r_ref, o_ref, acc_ref):
    kk = pl.program_id(2)

    @pl.when(kk == 0)
    def _():
        acc_ref[...] = r_ref[...]

    acc_ref[...] += jnp.dot(a_ref[...], w_ref[...], preferred_element_type=F32)

    @pl.when(kk == pl.num_programs(2) - 1)
    def _():
        o_ref[...] = acc_ref[...]


def _mm_res(a, w, r, *, tm, tn, tk):
    m, k = a.shape
    n = w.shape[1]
    tm = min(tm, m)
    return pl.pallas_call(
        _mm_res_kernel,
        grid=(m // tm, n // tn, k // tk),
        in_specs=[pl.BlockSpec((tm, tk), lambda i, j, q: (i, q)),
                  pl.BlockSpec((tk, tn), lambda i, j, q: (q, j)),
                  pl.BlockSpec((tm, tn), lambda i, j, q: (i, j))],
        out_specs=pl.BlockSpec((tm, tn), lambda i, j, q: (i, j)),
        out_shape=jax.ShapeDtypeStruct((m, n), F32),
        scratch_shapes=[pltpu.VMEM((tm, tn), F32)],
        compiler_params=_cparams(("arbitrary", "arbitrary", "arbitrary")),
        name="mm_res",
    )(a, w, r)


def _merge_kernel(ya_ref, yb_ref, yc_ref, yd_ref, ga_ref, gb_ref, gc_ref, gd_ref, wb_ref, o_ref):
    acc = None
    for br, (y_ref, g_ref) in enumerate(((ya_ref, ga_ref), (yb_ref, gb_ref), (yc_ref, gc_ref), (yd_ref, gd_ref))):
        t = _sigmoid(g_ref[...]) * jnp.dot(y_ref[...], wb_ref[br], preferred_element_type=F32)
        acc = t if acc is None else acc + t
    o_ref[...] = acc.astype(o_ref.dtype)


def _merge(ys, proj2d, wb, *, tm, tn):
    m = proj2d.shape[0]
    tm = min(tm, m)
    y_spec = pl.BlockSpec((tm, BRANCH_WIDTH), lambda i, j: (i, 0))
    gate_specs = [pl.BlockSpec((tm, tn), functools.partial(
        lambda i, j, off: (i, off + j), off=(COL_GATE + br * D_MODEL) // tn)) for br in range(N_BRANCH)]
    return pl.pallas_call(
        _merge_kernel,
        grid=(m // tm, D_MODEL // tn),
        in_specs=[y_spec] * N_BRANCH + gate_specs + [pl.BlockSpec((N_BRANCH, BRANCH_WIDTH, tn), lambda i, j: (0, 0, j))],
        out_specs=pl.BlockSpec((tm, tn), lambda i, j: (i, j)),
        out_shape=jax.ShapeDtypeStruct((m, D_MODEL), BF16),
        compiler_params=_cparams(("arbitrary", "arbitrary")),
        name="merge",
    )(*ys, proj2d, proj2d, proj2d, proj2d, wb)


def _final_norm_kernel(x_ref, nw_ref, o_ref):
    x = x_ref[...]
    ms = jnp.mean(x * x, axis=-1, keepdims=True)
    o_ref[...] = x * lax.rsqrt(ms + NORM_EPS) * nw_ref[...]


def _final_norm(x, nw, *, tm):
    m, k = x.shape
    tm = min(tm, m)
    return pl.pallas_call(
        _final_norm_kernel,
        grid=(m // tm,),
        in_specs=[pl.BlockSpec((tm, k), lambda i: (i, 0)), pl.BlockSpec((1, k), lambda i: (0, 0))],
        out_specs=pl.BlockSpec((tm, k), lambda i: (i, 0)),
        out_shape=jax.ShapeDtypeStruct((m, k), F32),
        compiler_params=_cparams(("arbitrary",)),
        name="final_norm",
    )(x, nw)


def _seg_spec(nb, t, width, col):
    return pl.BlockSpec((nb, t, width), functools.partial(lambda i, c, cb: (i, c, cb), cb=col // width))


def _const_spec(shape):
    return pl.BlockSpec(shape, lambda i, c: (0,) * len(shape))


def _ssd_kernel(xbc_ref, z_ref, sm_ref, cw_ref, cb_ref, dtb_ref, alog_ref, dsk_ref, nw_ref, e_ref,
                o_ref, ext_ref, state_ref):
    t = SSM_CHUNK
    gw = BRANCH_WIDTH // SSM_GROUPS
    heads_per_group = SSM_HEADS // SSM_GROUPS

    @pl.when(pl.program_id(1) == 0)
    def _():
        ext_ref[:, 0:SUBLANES, :] = jnp.zeros((ext_ref.shape[0], SUBLANES, ext_ref.shape[2]), F32)
        state_ref[...] = jnp.zeros_like(state_ref)

    causal = _tril01(t)
    causal16 = causal.astype(BF16)
    e01 = e_ref[...]
    lane_hi = lax.broadcasted_iota(jnp.int32, (1, LANES), 1) >= SSM_HEAD_DIM
    neg_a = -jnp.exp(alog_ref[...])
    for s in range(xbc_ref.shape[0]):
        u = _silu(_causal_conv(xbc_ref[s], ext_ref, s, cw_ref[...], cb_ref[...]))
        xs = u[:, :BRANCH_WIDTH]
        bm = u[:, BRANCH_WIDTH:BRANCH_WIDTH + SSM_GROUPS * SSM_STATE]
        cm = u[:, BRANCH_WIDTH + SSM_GROUPS * SSM_STATE:]

        dt = _softplus(sm_ref[s] + dtb_ref[...])
        cs = _dot01_left(causal16, dt * neg_a)
        cs_t = cs.T
        dt_t = dt.T
        tot = cs[t - 1:t]
        w_exp = _dot01_right(jnp.exp(tot - cs) * dt, e01)
        ecs_exp = _dot01_right(jnp.exp(cs), e01)
        etot_exp = _dot01_right(jnp.broadcast_to(jnp.exp(tot), (SUBLANES, LANES)), e01)[0:1]

        z = z_ref[s]
        for g in range(SSM_GROUPS):
            gsl = slice(g * gw, (g + 1) * gw)
            bg = bm[:, g * SSM_STATE:(g + 1) * SSM_STATE]
            bg16 = bg.astype(BF16)
            cg16 = cm[:, g * SSM_STATE:(g + 1) * SSM_STATE].astype(BF16)
            cb_mat = _dot_nt(cg16, bg16)
            st = state_ref[s, g]
            xg = xs[:, gsl]
            y = jnp.dot(cg16, st.astype(BF16), preferred_element_type=F32) * ecs_exp[:, gsl]
            pairs = []
            for pr in range(gw // LANES):
                xpair = xg[:, pr * LANES:(pr + 1) * LANES]
                ypair = None
                for hh in range(LANES // SSM_HEAD_DIM):
                    h = g * heads_per_group + pr * (LANES // SSM_HEAD_DIM) + hh
                    seg = jnp.minimum(cs[:, h:h + 1] - cs_t[h:h + 1, :], 0.0)
                    lmat = jnp.where(causal, jnp.exp(seg), 0.0)
                    mh = (cb_mat * lmat * dt_t[h:h + 1, :]).astype(BF16)
                    xm = jnp.where(lane_hi if hh else jnp.logical_not(lane_hi), xpair, 0.0).astype(BF16)
                    part = jnp.dot(mh, xm, preferred_element_type=F32)
                    ypair = part if ypair is None else ypair + part
                pairs.append(ypair)
            y = y + jnp.concatenate(pairs, axis=1) + dsk_ref[:, gsl] * xg
            xw = (xg * w_exp[:, gsl]).astype(BF16)
            state_ref[s, g] = st * etot_exp[:, gsl] + jnp.dot(bg.T.astype(BF16), xw, preferred_element_type=F32)
            y = y * _silu(z[:, gsl])
            ms = jnp.mean(y * y, axis=-1, keepdims=True)
            o_ref[s, :, gsl] = (y * lax.rsqrt(ms + NORM_EPS) * nw_ref[:, gsl]).astype(o_ref.dtype)


def _ssd(proj, small, cw, cb, dtb, alog, dsk, nw, e01):
    b, l, _ = proj.shape
    t = SSM_CHUNK
    nb = min(SEQS_PER_STEP, b)
    return pl.pallas_call(
        _ssd_kernel,
        grid=(b // nb, l // t),
        in_specs=[_seg_spec(nb, t, SSM_CONV_DIM, COL_XBC), _seg_spec(nb, t, BRANCH_WIDTH, COL_Z),
                  pl.BlockSpec((nb, t, LANES), lambda i, c: (i, c, 0)),
                  _const_spec((CONV_WIDTH, SSM_CONV_DIM)), _const_spec((1, SSM_CONV_DIM)),
                  _const_spec((1, LANES)), _const_spec((1, LANES)),
                  _const_spec((1, BRANCH_WIDTH)), _const_spec((1, BRANCH_WIDTH)), _const_spec((LANES, BRANCH_WIDTH))],
        out_specs=pl.BlockSpec((nb, t, BRANCH_WIDTH), lambda i, c: (i, c, 0)),
        out_shape=jax.ShapeDtypeStruct((b, l, BRANCH_WIDTH), BF16),
        scratch_shapes=[pltpu.VMEM((nb, SUBLANES + t, SSM_CONV_DIM), F32),
                        pltpu.VMEM((nb, SSM_GROUPS, SSM_STATE, BRANCH_WIDTH // SSM_GROUPS), F32)],
        compiler_params=_cparams(("arbitrary", "arbitrary")),
        name="ssd",
    )(proj, proj, small, cw, cb, dtb, alog, dsk, nw, e01)


def _hgrn_kernel(q_ref, f_ref, i_ref, g_ref, lbl_ref, nw_ref, o_ref, st_ref, k_s, gc_s, *, layer):
    t = HGRN_CHUNK
    sub = HGRN_SUB
    half = SUBLANES
    hd = HGRN_HEAD_DIM

    @pl.when(pl.program_id(1) == 0)
    def _():
        st_ref[...] = jnp.zeros_like(st_ref)

    logits = lbl_ref[...]
    ex = jnp.exp(logits - jnp.max(logits, axis=0, keepdims=True))
    sm = ex / jnp.sum(ex, axis=0, keepdims=True)
    lb = jnp.zeros((1, BRANCH_WIDTH), F32)
    for r in range(1, layer + 1):
        lb = lb + sm[r:r + 1]
    la = jnp.log(lb)
    l1 = jnp.log1p(-lb)

    tril16 = _tril01(t).astype(BF16)
    ones16 = jnp.ones((hd, hd), BF16)
    lane_h = lax.broadcasted_iota(jnp.int32, (half, t), 1)
    row_h = lax.broadcasted_iota(jnp.int32, (half, t), 0)
    lane_s = lax.broadcasted_iota(jnp.int32, (sub, t), 1)
    for s in range(q_ref.shape[0]):
        fp = f_ref[s]
        lbb = l1 + _log_sigmoid(fp)
        logf = jnp.maximum(la, lbb) + jnp.log1p(jnp.exp(-jnp.abs(la - lbb)))
        k_s[s] = (1.0 - lb) * _sigmoid(-fp)
        gc_s[s] = _dot01_left(tril16, logf) * LOG2E
        qq = _silu(q_ref[s]) * (hd ** -0.5)
        vv = i_ref[s]
        gg = g_ref[s]
        for h in range(HGRN_HEADS):
            sl = slice(h * hd, (h + 1) * hd)
            q = qq[:, sl]
            v16 = vv[:, sl].astype(BF16)
            k = k_s[s, :, sl]
            gc = gc_s[s, :, sl]
            glast = gc_s[s, t - 1:t, sl]
            st = st_ref[s, h]
            o = _dot_nt((q * jnp.exp2(gc)).astype(BF16), st.astype(BF16))
            a_rows = []
            for sc in range(t // sub):
                r0 = sc * sub
                q_lo, q_hi = q[r0:r0 + half], q[r0 + half:r0 + sub]
                g_lo, g_hi = gc[r0:r0 + half], gc[r0 + half:r0 + sub]
                prods = []
                for j in range(sub):
                    k_j = k_s[s, r0 + j:r0 + j + 1, sl]
                    g_j = gc_s[s, r0 + j:r0 + j + 1, sl]
                    if j < half:
                        prods.append(q_lo * k_j * jnp.exp2(jnp.minimum(g_lo - g_j, 0.0)))
                        prods.append(q_hi * k_j * jnp.exp2(g_hi - g_j))
                    else:
                        prods.append(q_hi * k_j * jnp.exp2(jnp.minimum(g_hi - g_j, 0.0)))
                rsum = jnp.dot(jnp.concatenate(prods, axis=0).astype(BF16), ones16, preferred_element_type=F32)
                blk_lo = jnp.zeros((half, t), F32)
                blk_hi = jnp.zeros((half, t), F32)
                idx = 0
                for j in range(sub):
                    hit = lane_h == r0 + j
                    if j < half:
                        blk_lo = jnp.where(hit & (row_h >= j), rsum[idx * half:(idx + 1) * half, :t], blk_lo)
                        blk_hi = jnp.where(hit, rsum[(idx + 1) * half:(idx + 2) * half, :t], blk_hi)
                        idx += 2
                    else:
                        blk_hi = jnp.where(hit & (row_h >= j - half), rsum[idx * half:(idx + 1) * half, :t], blk_hi)
                        idx += 1
                blk = jnp.concatenate([blk_lo, blk_hi], axis=0)
                if sc > 0:
                    g_r = gc_s[s, r0:r0 + 1, sl]
                    qs = (q[r0:r0 + sub] * jnp.exp2(gc[r0:r0 + sub] - g_r)).astype(BF16)
                    ks = (k * jnp.exp2(jnp.minimum(g_r - gc, 0.0))).astype(BF16)
                    blk = jnp.where(lane_s < r0, _dot_nt(qs, ks), blk)
                a_rows.append(blk)
            attn = jnp.concatenate(a_rows, axis=0)
            o = o + jnp.dot(attn.astype(BF16), v16, preferred_element_type=F32)
            kd = (k * jnp.exp2(glast - gc)).astype(BF16)
            st_ref[s, h] = st * jnp.exp2(glast) + jnp.dot(vv[:, sl].T.astype(BF16), kd, preferred_element_type=F32)
            ms = jnp.mean(o * o, axis=-1, keepdims=True)
            o = o * lax.rsqrt(ms + NORM_EPS) * nw_ref[:, sl]
            o_ref[s, :, sl] = (o * _silu(gg[:, sl])).astype(o_ref.dtype)


def _hgrn(proj, lb_logits, nw, layer):
    b, l, _ = proj.shape
    t = HGRN_CHUNK
    w = BRANCH_WIDTH
    nb = min(SEQS_PER_STEP, b)
    return pl.pallas_call(
        functools.partial(_hgrn_kernel, layer=layer),
        grid=(b // nb, l // t),
        in_specs=[_seg_spec(nb, t, w, COL_BQ), _seg_spec(nb, t, w, COL_BF), _seg_spec(nb, t, w, COL_BI),
                  _seg_spec(nb, t, w, COL_BG), _const_spec(lb_logits.shape), _const_spec((1, w))],
        out_specs=pl.BlockSpec((nb, t, w), lambda i, c: (i, c, 0)),
        out_shape=jax.ShapeDtypeStruct((b, l, w), BF16),
        scratch_shapes=[pltpu.VMEM((nb, HGRN_HEADS, HGRN_HEAD_DIM, HGRN_HEAD_DIM), F32),
                        pltpu.VMEM((nb, t, w), F32), pltpu.VMEM((nb, t, w), F32)],
        compiler_params=_cparams(("arbitrary", "arbitrary")),
        name="hgrn2",
    )(proj, proj, proj, proj, lb_logits, nw)


def _mlstm_kernel(qk_ref, v_ref, og_ref, sm_ref, gb_ref, nw_ref, out_ref, c_ref, m_ref):
    t = MLSTM_CHUNK
    dk = MLSTM_QK_DIM
    dv = MLSTM_V_DIM

    @pl.when(pl.program_id(1) == 0)
    def _():
        c_ref[...] = jnp.zeros_like(c_ref)
        m_ref[...] = jnp.zeros_like(m_ref)

    causal = _tril01(t)
    causal16 = causal.astype(BF16)
    eye = (lax.broadcasted_iota(jnp.int32, (LANES, LANES), 0)
           == lax.broadcasted_iota(jnp.int32, (LANES, LANES), 1)).astype(BF16)
    lane_e = lax.broadcasted_iota(jnp.int32, (t, LANES), 1)
    ones_col = jnp.where(lane_e == 0, 1.0, 0.0).astype(BF16)
    for s in range(qk_ref.shape[0]):
        gates = sm_ref[s] + gb_ref[...]
        lf = _log_sigmoid(gates)
        bcum = _dot01_left(causal16, lf)
        bcum_t = sum(_dot_nt(eye, p) for p in _split3(bcum))
        gates_t = sum(_dot_nt(eye, p) for p in _split3(gates))
        qk = qk_ref[s]
        vv = v_ref[s]
        og = og_ref[s]
        for h in range(MLSTM_HEADS):
            q16 = (qk[:, h * dk:(h + 1) * dk] * (dk ** -0.5)).astype(BF16)
            k = qk[:, MLSTM_HEADS * dk + h * dk:MLSTM_HEADS * dk + (h + 1) * dk]
            v_ext = jnp.concatenate([vv[:, h * dv:(h + 1) * dv].astype(BF16), ones_col], axis=1)
            bc = bcum[:, SM_CF + h:SM_CF + h + 1]
            ig = gates[:, SM_CI + h:SM_CI + h + 1]
            bc_r = bcum_t[SM_CF + h:SM_CF + h + 1, :]
            ig_r = gates_t[SM_CI + h:SM_CI + h + 1, :]
            m_st = m_ref[s, h][:, 0:1]
            c_st = c_ref[s, h]

            dlog = jnp.where(causal, bc - bc_r + ig_r, -jnp.inf)
            inter_log = bc + m_st
            m = jnp.maximum(inter_log, jnp.max(dlog, axis=-1, keepdims=True))
            w_intra = jnp.exp(dlog - m)
            w_inter = jnp.exp(inter_log - m)
            sc = _dot_nt(q16, k.astype(BF16)) * w_intra
            nd = jnp.dot(sc.astype(BF16), v_ext, preferred_element_type=F32) \
                + w_inter * jnp.dot(q16, c_st.astype(BF16), preferred_element_type=F32)
            num = nd[:, :dv]
            den = nd[:, dv:dv + 1]
            hh = num / jnp.maximum(jnp.abs(den), jnp.exp(-m))

            b_last = bcum[t - 1:t, SM_CF + h:SM_CF + h + 1]
            log_w = b_last - bc + ig
            m_new = jnp.maximum(b_last + m_st, jnp.max(log_w, axis=0, keepdims=True))
            wk = jnp.exp(log_w - m_new)
            decay = jnp.exp(b_last + m_st - m_new)
            kv = jnp.dot(k.T.astype(BF16), (wk * v_ext.astype(F32)).astype(BF16), preferred_element_type=F32)
            c_ref[s, h] = decay * c_st + kv
            m_ref[s, h] = jnp.broadcast_to(m_new, (1, LANES))

            ms = jnp.mean(hh * hh, axis=-1, keepdims=True)
            hn = hh * lax.rsqrt(ms + NORM_EPS) * nw_ref[:, h * dv:(h + 1) * dv]
            out_ref[s, :, h * dv:(h + 1) * dv] = (hn * _sigmoid(og[:, h * dv:(h + 1) * dv])).astype(out_ref.dtype)


def _mlstm(proj, small, gate_bias, nw):
    b, l, _ = proj.shape
    t = MLSTM_CHUNK
    w = BRANCH_WIDTH
    nb = min(SEQS_PER_STEP, b)
    return pl.pallas_call(
        _mlstm_kernel,
        grid=(b // nb, l // t),
        in_specs=[_seg_spec(nb, t, w, COL_CQK), _seg_spec(nb, t, w, COL_CV), _seg_spec(nb, t, w, COL_CO),
                  pl.BlockSpec((nb, t, LANES), lambda i, c: (i, c, 0)),
                  _const_spec((1, LANES)), _const_spec((1, w))],
        out_specs=pl.BlockSpec((nb, t, w), lambda i, c: (i, c, 0)),
        out_shape=jax.ShapeDtypeStruct((b, l, w), BF16),
        scratch_shapes=[pltpu.VMEM((nb, MLSTM_HEADS, MLSTM_QK_DIM, MLSTM_V_DIM + LANES), F32),
                        pltpu.VMEM((nb, MLSTM_HEADS, 1, LANES), F32)],
        compiler_params=_cparams(("arbitrary", "arbitrary")),
        name="mlstm",
    )(proj, proj, proj, small, gate_bias, nw)


def _lru_kernel(x_ref, g_ref, cw_ref, cb_ref, wa_ref, ba_ref, wi_ref, bi_ref, ap_ref, o_ref, ext_ref, h_ref):
    t = LRU_CHUNK
    bd = LRU_BLOCK_DIM

    @pl.when(pl.program_id(1) == 0)
    def _():
        ext_ref[:, 0:SUBLANES, :] = jnp.zeros((ext_ref.shape[0], SUBLANES, ext_ref.shape[2]), F32)
        h_ref[...] = jnp.zeros_like(h_ref)

    row_in = lax.broadcasted_iota(jnp.int32, (SUBLANES, 1), 0)
    neg_sp = -LRU_C * _softplus(-ap_ref[...])
    for s in range(x_ref.shape[0]):
        xc = _causal_conv(x_ref[s], ext_ref, s, cw_ref[...], cb_ref[...])
        xc16 = xc.astype(BF16)
        r_parts, i_parts = [], []
        for n in range(LRU_BLOCKS):
            xb = xc16[:, n * bd:(n + 1) * bd]
            r_parts.append(jnp.dot(xb, wa_ref[n], preferred_element_type=F32))
            i_parts.append(jnp.dot(xb, wi_ref[n], preferred_element_type=F32))
        r = _sigmoid(jnp.concatenate(r_parts, axis=1) + ba_ref[...])
        ig = _sigmoid(jnp.concatenate(i_parts, axis=1) + bi_ref[...])
        log_a = r * neg_sp
        a = jnp.exp(log_a)
        var = -jnp.tanh(log_a) * (a * a + 1.0)
        u = xc * ig * jnp.where(var > 0.0, var * lax.rsqrt(var), 0.0)

        carry = h_ref[s, 0:1]
        groups = []
        for r0 in range(0, t, SUBLANES):
            ug = u[r0:r0 + SUBLANES]
            ag = a[r0:r0 + SUBLANES]
            d = 1
            while d < SUBLANES:
                keep = row_in >= d
                ug = ug + ag * jnp.where(keep, pltpu.roll(ug, d, 0), 0.0)
                ag = ag * jnp.where(keep, pltpu.roll(ag, d, 0), 1.0)
                d *= 2
            hg = ug + ag * carry
            carry = hg[SUBLANES - 1:SUBLANES]
            groups.append(hg)
        h_ref[s] = jnp.broadcast_to(carry, h_ref.shape[1:])
        hseq = jnp.concatenate(groups, axis=0)
        o_ref[s] = (hseq * jax.nn.gelu(g_ref[s], approximate=True)).astype(o_ref.dtype)


def _lru(proj, cw, cb, wa, ba, wi, bi, ap):
    b, l, _ = proj.shape
    t = LRU_CHUNK
    w = BRANCH_WIDTH
    nb = min(SEQS_PER_STEP, b)
    row = _const_spec((1, w))
    blk = _const_spec((LRU_BLOCKS, LRU_BLOCK_DIM, LRU_BLOCK_DIM))
    return pl.pallas_call(
        _lru_kernel,
        grid=(b // nb, l // t),
        in_specs=[_seg_spec(nb, t, w, COL_DX), _seg_spec(nb, t, w, COL_DG), _const_spec((CONV_WIDTH, w)), row,
                  blk, row, blk, row, row],
        out_specs=pl.BlockSpec((nb, t, w), lambda i, c: (i, c, 0)),
        out_shape=jax.ShapeDtypeStruct((b, l, w), BF16),
        scratch_shapes=[pltpu.VMEM((nb, SUBLANES + t, w), F32), pltpu.VMEM((nb, SUBLANES, w), F32)],
        compiler_params=_cparams(("arbitrary", "arbitrary")),
        name="rglru",
    )(proj, proj, cw, cb, wa, ba, wi, bi, ap)


def _regroup_w_in(w):
    widths = (BRANCH_WIDTH, SSM_CONV_DIM, SSM_HEADS, BRANCH_WIDTH, BRANCH_WIDTH, BRANCH_WIDTH, BRANCH_WIDTH,
              MLSTM_HEADS * MLSTM_QK_DIM, MLSTM_HEADS * MLSTM_QK_DIM, BRANCH_WIDTH, BRANCH_WIDTH,
              MLSTM_HEADS, MLSTM_HEADS, BRANCH_WIDTH, BRANCH_WIDTH, N_BRANCH * D_MODEL)
    offs = [0]
    for wd in widths:
        offs.append(offs[-1] + wd)
    seg = [w[:, offs[i]:offs[i + 1]] for i in range(len(widths))]
    (a_z, a_xbc, a_dt, b_q, b_f, b_i, b_g, c_q, c_k, c_v, c_o, c_i, c_f, d_x, d_g, gate) = seg
    main = jnp.concatenate([a_xbc, a_z, b_q, b_f, b_i, b_g, c_q, c_k, c_v, c_o, d_x, d_g, gate], axis=1)
    pad = jnp.zeros((w.shape[0], LANES - SSM_HEADS - 2 * MLSTM_HEADS), w.dtype)
    small = jnp.concatenate([a_dt, c_i, c_f, pad], axis=1)
    return main.astype(BF16), small.astype(BF16)


def _pad_lanes(v, start):
    out = jnp.zeros((1, LANES), F32)
    return out.at[0, start:start + v.shape[0]].set(v.astype(F32))


def kernel(x, p, mix_norm, w_in, ssm_conv_w, ssm_conv_b, ssm_dt_bias, ssm_a_log, ssm_d, ssm_norm, hgrn_lb_logits, hgrn_norm, mlstm_i_bias, mlstm_f_bias, mlstm_norm, lru_conv_w, lru_conv_b, lru_wa, lru_ba, lru_wi, lru_bi, lru_a_param, w_branch, w_out, mlp_norm, w_up, w_down, ple_norm, w_ple, w_ple_gate, final_norm):
    b, l, d = x.shape
    depth = w_in.shape[0]
    n = b * l
    row = lambda v: v.astype(F32).reshape(1, -1)
    e01 = (jnp.arange(LANES)[:, None] == (jnp.arange(BRANCH_WIDTH)[None, :] // SSM_HEAD_DIM)).astype(BF16)
    w_branch16, w_out16, w_up16, w_down16 = (w.astype(BF16) for w in (w_branch, w_out, w_up, w_down))
    w_ple16, w_ple_gate16, lru_wa16, lru_wi16 = (w.astype(BF16) for w in (w_ple, w_ple_gate, lru_wa, lru_wi))

    xf = x.reshape(n, d)
    for i in range(depth):
        w_main, w_small = _regroup_w_in(w_in[i])
        proj, small = _inproj(xf, row(mix_norm[i]), w_main, w_small, tm=1024, tn=1024)
        proj3 = proj.reshape(b, l, P_COLS)
        small3 = small.reshape(b, l, LANES)

        y_a = _ssd(proj3, small3, ssm_conv_w[i].astype(F32), row(ssm_conv_b[i]), _pad_lanes(ssm_dt_bias[i], SM_DT),
                   _pad_lanes(ssm_a_log[i], SM_DT), row(jnp.repeat(ssm_d[i], SSM_HEAD_DIM)), row(ssm_norm[i]), e01)
        y_b = _hgrn(proj3, hgrn_lb_logits.astype(F32), row(hgrn_norm[i]), i)
        gate_bias = _pad_lanes(jnp.concatenate([mlstm_i_bias[i], mlstm_f_bias[i]]), SM_CI)
        y_c = _mlstm(proj3, small3, gate_bias, row(mlstm_norm[i]))
        y_d = _lru(proj3, lru_conv_w[i].astype(F32), row(lru_conv_b[i]), lru_wa16[i], row(lru_ba[i]),
                   lru_wi16[i], row(lru_bi[i]), row(lru_a_param[i]))

        ys = [y.reshape(n, BRANCH_WIDTH) for y in (y_a, y_b, y_c, y_d)]
        merged = _merge(ys, proj, w_branch16[i], tm=1024, tn=512)
        xf = _mm_res(merged, w_out16[i], xf, tm=1024, tn=1024, tk=D_MODEL)

        up = _up(xf, row(mlp_norm[i]), w_up16[i], tm=1024, tn=1024)
        xf = _mm_res(up, w_down16[i], xf, tm=1024, tn=1024, tk=2048)

        xf = _ple(xf, row(ple_norm[i]), w_ple_gate16[i], p[i].reshape(n, PLE_DIM), w_ple16[i], tm=1024, tn=1024)
    return _final_norm(xf, row(final_norm), tm=512).reshape(b, l, d)
```

```python
import functools
import math

import jax
import jax.numpy as jnp
from jax import lax
from jax.experimental import pallas as pl
from jax.experimental.pallas import tpu as pltpu

F32 = jnp.float32
BF16 = jnp.bfloat16

D_MODEL = 2048
NORM_EPS = 1e-6
PLE_DIM = 256
N_BRANCH = 4
BRANCH_WIDTH = D_MODEL // 2
CONV_WIDTH = 4
SSM_HEAD_DIM = 64
SSM_HEADS = BRANCH_WIDTH // SSM_HEAD_DIM
SSM_GROUPS = 4
SSM_STATE = 128
SSM_CHUNK = 128
SSM_CONV_DIM = BRANCH_WIDTH + 2 * SSM_GROUPS * SSM_STATE
HGRN_HEAD_DIM = 128
HGRN_HEADS = BRANCH_WIDTH // HGRN_HEAD_DIM
HGRN_CHUNK = 64
HGRN_SUB = 16
MLSTM_HEADS = 4
MLSTM_QK_DIM = BRANCH_WIDTH // (2 * MLSTM_HEADS)
MLSTM_V_DIM = BRANCH_WIDTH // MLSTM_HEADS
MLSTM_CHUNK = 64
LRU_BLOCKS = 8
LRU_BLOCK_DIM = BRANCH_WIDTH // LRU_BLOCKS
LRU_C = 8.0
LRU_CHUNK = 128
D_FF = 4 * D_MODEL

LANES = 128
SUBLANES = 8
VMEM_LIMIT = 48 * 1024 * 1024
LOG2E = math.log2(math.e)
LN2 = math.log(2.0)
SEQS_PER_STEP = 2

COL_XBC = 0
COL_Z = COL_XBC + SSM_CONV_DIM
COL_BQ = COL_Z + BRANCH_WIDTH
COL_BF = COL_BQ + BRANCH_WIDTH
COL_BI = COL_BF + BRANCH_WIDTH
COL_BG = COL_BI + BRANCH_WIDTH
COL_CQK = COL_BG + BRANCH_WIDTH
COL_CV = COL_CQK + BRANCH_WIDTH
COL_CO = COL_CV + BRANCH_WIDTH
COL_DX = COL_CO + BRANCH_WIDTH
COL_DG = COL_DX + BRANCH_WIDTH
COL_GATE = COL_DG + BRANCH_WIDTH
P_COLS = COL_GATE + N_BRANCH * D_MODEL
SM_DT = 0
SM_CI = SSM_HEADS
SM_CF = SSM_HEADS + MLSTM_HEADS


def _sigmoid(x):
    return 1.0 / (1.0 + jnp.exp(-x))


def _silu(x):
    return x * _sigmoid(x)


def _softplus(x):
    return jnp.maximum(x, 0.0) + LN2 * jnp.log2(1.0 + jnp.exp2(-LOG2E * jnp.abs(x)))


def _log_sigmoid(x):
    return jnp.minimum(x, 0.0) - LN2 * jnp.log2(1.0 + jnp.exp2(-LOG2E * jnp.abs(x)))


def _split3(x):
    hi = x.astype(BF16)
    r1 = x - hi.astype(F32)
    mid = r1.astype(BF16)
    lo = (r1 - mid.astype(F32)).astype(BF16)
    return hi, mid, lo


def _dot01_left(a01, x):
    return sum(jnp.dot(a01, p, preferred_element_type=F32) for p in _split3(x))


def _dot01_right(x, a01):
    return sum(jnp.dot(p, a01, preferred_element_type=F32) for p in _split3(x))


def _dot_nt(a, b):
    return lax.dot_general(a, b, (((1,), (1,)), ((), ())), preferred_element_type=F32)


def _tril01(t):
    r = lax.broadcasted_iota(jnp.int32, (t, t), 0)
    c = lax.broadcasted_iota(jnp.int32, (t, t), 1)
    return r >= c


def _causal_conv(x, ext_ref, s, cw, cb):
    t = x.shape[0]
    ext_ref[s, SUBLANES:SUBLANES + t, :] = x
    acc = cb + cw[CONV_WIDTH - 1:CONV_WIDTH] * x
    for k in range(1, CONV_WIDTH):
        acc = acc + cw[CONV_WIDTH - 1 - k:CONV_WIDTH - k] * ext_ref[s, SUBLANES - k:SUBLANES - k + t, :]
    ext_ref[s, 0:SUBLANES, :] = x[t - SUBLANES:t]
    return acc


def _cparams(sem):
    return pltpu.CompilerParams(dimension_semantics=sem, vmem_limit_bytes=VMEM_LIMIT)


def _rms_to_scratch(x_ref, nw_ref, h_ref):
    x = x_ref[...]
    ms = jnp.mean(x * x, axis=-1, keepdims=True)
    h_ref[...] = (x * lax.rsqrt(ms + NORM_EPS) * nw_ref[...]).astype(BF16)


def _inproj_kernel(x_ref, nw_ref, w_ref, ws_ref, o_ref, os_ref, h_ref):
    @pl.when(pl.program_id(1) == 0)
    def _():
        _rms_to_scratch(x_ref, nw_ref, h_ref)
        os_ref[...] = jnp.dot(h_ref[...], ws_ref[...], preferred_element_type=F32)

    o_ref[...] = jnp.dot(h_ref[...], w_ref[...], preferred_element_type=F32)


def _inproj(x, nw, w, ws, *, tm, tn):
    m, k = x.shape
    n = w.shape[1]
    tm = min(tm, m)
    return pl.pallas_call(
        _inproj_kernel,
        grid=(m // tm, n // tn),
        in_specs=[pl.BlockSpec((tm, k), lambda i, j: (i, 0)),
                  pl.BlockSpec((1, k), lambda i, j: (0, 0)),
                  pl.BlockSpec((k, tn), lambda i, j: (0, j)),
                  pl.BlockSpec((k, LANES), lambda i, j: (0, 0))],
        out_specs=[pl.BlockSpec((tm, tn), lambda i, j: (i, j)),
                   pl.BlockSpec((tm, LANES), lambda i, j: (i, 0))],
        out_shape=[jax.ShapeDtypeStruct((m, n), F32), jax.ShapeDtypeStruct((m, LANES), F32)],
        scratch_shapes=[pltpu.VMEM((tm, k), BF16)],
        compiler_params=_cparams(("arbitrary", "arbitrary")),
        name="inproj",
    )(x, nw, w, ws)


def _up_kernel(x_ref, nw_ref, w_ref, o_ref, h_ref):
    @pl.when(pl.program_id(1) == 0)
    def _():
        _rms_to_scratch(x_ref, nw_ref, h_ref)

    acc = jnp.dot(h_ref[...], w_ref[...], preferred_element_type=F32)
    o_ref[...] = jnp.square(jnp.maximum(acc, 0.0)).astype(o_ref.dtype)


def _up(x, nw, w, *, tm, tn):
    m, k = x.shape
    n = w.shape[1]
    tm = min(tm, m)
    return pl.pallas_call(
        _up_kernel,
        grid=(m // tm, n // tn),
        in_specs=[pl.BlockSpec((tm, k), lambda i, j: (i, 0)),
                  pl.BlockSpec((1, k), lambda i, j: (0, 0)),
                  pl.BlockSpec((k, tn), lambda i, j: (0, j))],
        out_specs=pl.BlockSpec((tm, tn), lambda i, j: (i, j)),
        out_shape=jax.ShapeDtypeStruct((m, n), BF16),
        scratch_shapes=[pltpu.VMEM((tm, k), BF16)],
        compiler_params=_cparams(("arbitrary", "arbitrary")),
        name="mlp_up",
    )(x, nw, w)


def _ple_kernel(x_ref, nw_ref, wg_ref, p_ref, wp_ref, o_ref, h_ref, *, tn):
    j = pl.program_id(1)

    @pl.when(j == 0)
    def _():
        _rms_to_scratch(x_ref, nw_ref, h_ref)

    gate = jnp.dot(h_ref[...], wg_ref[...], preferred_element_type=F32)
    emb = jnp.dot(p_ref[...].astype(BF16), wp_ref[...], preferred_element_type=F32)
    xc = x_ref[:, pl.ds(pl.multiple_of(j * tn, tn), tn)]
    o_ref[...] = xc + emb * _sigmoid(gate)


def _ple(x, nw, wg, p, wp, *, tm, tn):
    m, k = x.shape
    tm = min(tm, m)
    kp = p.shape[1]
    return pl.pallas_call(
        functools.partial(_ple_kernel, tn=tn),
        grid=(m // tm, k // tn),
        in_specs=[pl.BlockSpec((tm, k), lambda i, j: (i, 0)),
                  pl.BlockSpec((1, k), lambda i, j: (0, 0)),
                  pl.BlockSpec((k, tn), lambda i, j: (0, j)),
                  pl.BlockSpec((tm, kp), lambda i, j: (i, 0)),
                  pl.BlockSpec((kp, tn), lambda i, j: (0, j))],
        out_specs=pl.BlockSpec((tm, tn), lambda i, j: (i, j)),
        out_shape=jax.ShapeDtypeStruct((m, k), F32),
        scratch_shapes=[pltpu.VMEM((tm, k), BF16)],
        compiler_params=_cparams(("arbitrary", "arbitrary")),
        name="ple",
    )(x, nw, wg, p, wp)


def _mm_res_kernel(a_ref, w_ref, r_ref, o_ref, acc_ref):
    kk = pl.program_id(2)

    @pl.when(kk == 0)
    def _():
        acc_ref[...] = r_ref[...]

    acc_ref[...] += jnp.dot(a_ref[...], w_ref[...], preferred_element_type=F32)

    @pl.when(kk == pl.num_programs(2) - 1)
    def _():
        o_ref[...] = acc_ref[...]


def _mm_res(a, w, r, *, tm, tn, tk):
    m, k = a.shape
    n = w.shape[1]
    tm = min(tm, m)
    return pl.pallas_call(
        _mm_res_kernel,
        grid=(m // tm, n // tn, k // tk),
        in_specs=[pl.BlockSpec((tm, tk), lambda i, j, q: (i, q)),
                  pl.BlockSpec((tk, tn), lambda i, j, q: (q, j)),
                  pl.BlockSpec((tm, tn), lambda i, j, q: (i, j))],
        out_specs=pl.BlockSpec((tm, tn), lambda i, j, q: (i, j)),
        out_shape=jax.ShapeDtypeStruct((m, n), F32),
        scratch_shapes=[pltpu.VMEM((tm, tn), F32)],
        compiler_params=_cparams(("arbitrary", "arbitrary", "arbitrary")),
        name="mm_res",
    )(a, w, r)


def _merge_kernel(ya_ref, yb_ref, yc_ref, yd_ref, ga_ref, gb_ref, gc_ref, gd_ref, wb_ref, o_ref):
    acc = None
    for br, (y_ref, g_ref) in enumerate(((ya_ref, ga_ref), (yb_ref, gb_ref), (yc_ref, gc_ref), (yd_ref, gd_ref))):
        t = _sigmoid(g_ref[...]) * jnp.dot(y_ref[...], wb_ref[br], preferred_element_type=F32)
        acc = t if acc is None else acc + t
    o_ref[...] = acc.astype(o_ref.dtype)


def _merge(ys, proj2d, wb, *, tm, tn):
    m = proj2d.shape[0]
    tm = min(tm, m)
    y_spec = pl.BlockSpec((tm, BRANCH_WIDTH), lambda i, j: (i, 0))
    gate_specs = [pl.BlockSpec((tm, tn), functools.partial(
        lambda i, j, off: (i, off + j), off=(COL_GATE + br * D_MODEL) // tn)) for br in range(N_BRANCH)]
    return pl.pallas_call(
        _merge_kernel,
        grid=(m // tm, D_MODEL // tn),
        in_specs=[y_spec] * N_BRANCH + gate_specs + [pl.BlockSpec((N_BRANCH, BRANCH_WIDTH, tn), lambda i, j: (0, 0, j))],
        out_specs=pl.BlockSpec((tm, tn), lambda i, j: (i, j)),
        out_shape=jax.ShapeDtypeStruct((m, D_MODEL), BF16),
        compiler_params=_cparams(("arbitrary", "arbitrary")),
        name="merge",
    )(*ys, proj2d, proj2d, proj2d, proj2d, wb)


def _final_norm_kernel(x_ref, nw_ref, o_ref):
    x = x_ref[...]
    ms = jnp.mean(x * x, axis=-1, keepdims=True)
    o_ref[...] = x * lax.rsqrt(ms + NORM_EPS) * nw_ref[...]


def _final_norm(x, nw, *, tm):
    m, k = x.shape
    tm = min(tm, m)
    return pl.pallas_call(
        _final_norm_kernel,
        grid=(m // tm,),
        in_specs=[pl.BlockSpec((tm, k), lambda i: (i, 0)), pl.BlockSpec((1, k), lambda i: (0, 0))],
        out_specs=pl.BlockSpec((tm, k), lambda i: (i, 0)),
        out_shape=jax.ShapeDtypeStruct((m, k), F32),
        compiler_params=_cparams(("arbitrary",)),
        name="final_norm",
    )(x, nw)


def _seg_spec(nb, t, width, col):
    return pl.BlockSpec((nb, t, width), functools.partial(lambda i, c, cb: (i, c, cb), cb=col // width))


def _const_spec(shape):
    return pl.BlockSpec(shape, lambda i, c: (0,) * len(shape))


def _ssd_kernel(xbc_ref, z_ref, sm_ref, cw_ref, cb_ref, dtb_ref, alog_ref, dsk_ref, nw_ref, e_ref,
                o_ref, ext_ref, state_ref):
    t = SSM_CHUNK
    gw = BRANCH_WIDTH // SSM_GROUPS
    heads_per_group = SSM_HEADS // SSM_GROUPS

    @pl.when(pl.program_id(1) == 0)
    def _():
        ext_ref[:, 0:SUBLANES, :] = jnp.zeros((ext_ref.shape[0], SUBLANES, ext_ref.shape[2]), F32)
        state_ref[...] = jnp.zeros_like(state_ref)

    causal = _tril01(t)
    causal16 = causal.astype(BF16)
    e01 = e_ref[...]
    lane_hi = lax.broadcasted_iota(jnp.int32, (1, LANES), 1) >= SSM_HEAD_DIM
    neg_a = -jnp.exp(alog_ref[...])
    for s in range(xbc_ref.shape[0]):
        u = _silu(_causal_conv(xbc_ref[s], ext_ref, s, cw_ref[...], cb_ref[...]))
        xs = u[:, :BRANCH_WIDTH]
        bm = u[:, BRANCH_WIDTH:BRANCH_WIDTH + SSM_GROUPS * SSM_STATE]
        cm = u[:, BRANCH_WIDTH + SSM_GROUPS * SSM_STATE:]

        dt = _softplus(sm_ref[s] + dtb_ref[...])
        cs = _dot01_left(causal16, dt * neg_a)
        cs_t = cs.T
        dt_t = dt.T
        tot = cs[t - 1:t]
        w_exp = _dot01_right(jnp.exp(tot - cs) * dt, e01)
        ecs_exp = _dot01_right(jnp.exp(cs), e01)
        etot_exp = _dot01_right(jnp.broadcast_to(jnp.exp(tot), (SUBLANES, LANES)), e01)[0:1]

        z = z_ref[s]
        for g in range(SSM_GROUPS):
            gsl = slice(g * gw, (g + 1) * gw)
            bg = bm[:, g * SSM_STATE:(g + 1) * SSM_STATE]
            bg16 = bg.astype(BF16)
            cg16 = cm[:, g * SSM_STATE:(g + 1) * SSM_STATE].astype(BF16)
            cb_mat = _dot_nt(cg16, bg16)
            st = state_ref[s, g]
            xg = xs[:, gsl]
            y = jnp.dot(cg16, st.astype(BF16), preferred_element_type=F32) * ecs_exp[:, gsl]
            pairs = []
            for pr in range(gw // LANES):
                xpair = xg[:, pr * LANES:(pr + 1) * LANES]
                ypair = None
                for hh in range(LANES // SSM_HEAD_DIM):
                    h = g * heads_per_group + pr * (LANES // SSM_HEAD_DIM) + hh
                    seg = jnp.minimum(cs[:, h:h + 1] - cs_t[h:h + 1, :], 0.0)
                    lmat = jnp.where(causal, jnp.exp(seg), 0.0)
                    mh = (cb_mat * lmat * dt_t[h:h + 1, :]).astype(BF16)
                    xm = jnp.where(lane_hi if hh else jnp.logical_not(lane_hi), xpair, 0.0).astype(BF16)
                    part = jnp.dot(mh, xm, preferred_element_type=F32)
                    ypair = part if ypair is None else ypair + part
                pairs.append(ypair)
            y = y + jnp.concatenate(pairs, axis=1) + dsk_ref[:, gsl] * xg
            xw = (xg * w_exp[:, gsl]).astype(BF16)
            state_ref[s, g] = st * etot_exp[:, gsl] + jnp.dot(bg.T.astype(BF16), xw, preferred_element_type=F32)
            y = y * _silu(z[:, gsl])
            ms = jnp.mean(y * y, axis=-1, keepdims=True)
            o_ref[s, :, gsl] = (y * lax.rsqrt(ms + NORM_EPS) * nw_ref[:, gsl]).astype(o_ref.dtype)


def _ssd(proj, small, cw, cb, dtb, alog, dsk, nw, e01):
    b, l, _ = proj.shape
    t = SSM_CHUNK
    nb = min(SEQS_PER_STEP, b)
    return pl.pallas_call(
        _ssd_kernel,
        grid=(b // nb, l // t),
        in_specs=[_seg_spec(nb, t, SSM_CONV_DIM, COL_XBC), _seg_spec(nb, t, BRANCH_WIDTH, COL_Z),
                  pl.BlockSpec((nb, t, LANES), lambda i, c: (i, c, 0)),
                  _const_spec((CONV_WIDTH, SSM_CONV_DIM)), _const_spec((1, SSM_CONV_DIM)),
                  _const_spec((1, LANES)), _const_spec((1, LANES)),
                  _const_spec((1, BRANCH_WIDTH)), _const_spec((1, BRANCH_WIDTH)), _const_spec((LANES, BRANCH_WIDTH))],
        out_specs=pl.BlockSpec((nb, t, BRANCH_WIDTH), lambda i, c: (i, c, 0)),
        out_shape=jax.ShapeDtypeStruct((b, l, BRANCH_WIDTH), BF16),
        scratch_shapes=[pltpu.VMEM((nb, SUBLANES + t, SSM_CONV_DIM), F32),
                        pltpu.VMEM((nb, SSM_GROUPS, SSM_STATE, BRANCH_WIDTH // SSM_GROUPS), F32)],
        compiler_params=_cparams(("arbitrary", "arbitrary")),
        name="ssd",
    )(proj, proj, small, cw, cb, dtb, alog, dsk, nw, e01)


def _hgrn_kernel(q_ref, f_ref, i_ref, g_ref, lbl_ref, nw_ref, o_ref, st_ref, k_s, gc_s, gk_s, lk_s, *, layer):
    t = HGRN_CHUNK
    sub = HGRN_SUB
    half = SUBLANES
    hd = HGRN_HEAD_DIM

    @pl.when(pl.program_id(1) == 0)
    def _():
        st_ref[...] = jnp.zeros_like(st_ref)

    logits = lbl_ref[...]
    ex = jnp.exp(logits - jnp.max(logits, axis=0, keepdims=True))
    sm = ex / jnp.sum(ex, axis=0, keepdims=True)
    lb = jnp.zeros((1, BRANCH_WIDTH), F32)
    for r in range(1, layer + 1):
        lb = lb + sm[r:r + 1]
    la = jnp.log(lb)
    l1 = jnp.log1p(-lb)

    tril16 = _tril01(t).astype(BF16)
    ones16 = jnp.ones((hd, hd), BF16)
    lane_h = lax.broadcasted_iota(jnp.int32, (half, t), 1)
    row_h = lax.broadcasted_iota(jnp.int32, (half, t), 0)
    lane_s = lax.broadcasted_iota(jnp.int32, (sub, t), 1)
    for s in range(q_ref.shape[0]):
        fp = f_ref[s]
        ls = _log_sigmoid(fp)
        lbb = l1 + ls
        logf = jnp.maximum(la, lbb) + LN2 * jnp.log2(1.0 + jnp.exp2(-LOG2E * jnp.abs(la - lbb)))
        k_s[s] = (1.0 - lb) * _sigmoid(-fp)
        lk = (l1 + (ls - fp)) * LOG2E
        gc2 = _dot01_left(tril16, logf) * LOG2E
        gc_s[s] = gc2
        gk_s[s] = gc2 - lk
        lk_s[s] = lk
        qq = _silu(q_ref[s]) * (hd ** -0.5)
        vv = i_ref[s]
        gg = g_ref[s]
        for h in range(HGRN_HEADS):
            sl = slice(h * hd, (h + 1) * hd)
            q = qq[:, sl]
            v16 = vv[:, sl].astype(BF16)
            k = k_s[s, :, sl]
            gc = gc_s[s, :, sl]
            glast = gc_s[s, t - 1:t, sl]
            st = st_ref[s, h]
            o = _dot_nt((q * jnp.exp2(gc)).astype(BF16), st.astype(BF16))
            g_ends = [gc_s[s, (sc + 1) * sub - 1:(sc + 1) * sub, sl] for sc in range(t // sub)]
            g_end_rows = jnp.concatenate([jnp.broadcast_to(g, (sub, hd)) for g in g_ends], axis=0)
            k_end = k * jnp.exp2(g_end_rows - gc)
            a_rows = []
            for sc in range(t // sub):
                r0 = sc * sub
                q_lo, q_hi = q[r0:r0 + half], q[r0 + half:r0 + sub]
                g_lo, g_hi = gc[r0:r0 + half], gc[r0 + half:r0 + sub]
                prods = []
                for j in range(sub):
                    gk_j = gk_s[s, r0 + j:r0 + j + 1, sl]
                    if j < half:
                        lk_j = lk_s[s, r0 + j:r0 + j + 1, sl]
                        prods.append(q_lo * jnp.exp2(jnp.minimum(g_lo - gk_j, lk_j)))
                        prods.append(q_hi * jnp.exp2(g_hi - gk_j))
                    else:
                        lk_j = lk_s[s, r0 + j:r0 + j + 1, sl]
                        prods.append(q_hi * jnp.exp2(jnp.minimum(g_hi - gk_j, lk_j)))
                rsum = jnp.dot(jnp.concatenate(prods, axis=0).astype(BF16), ones16, preferred_element_type=F32)
                blk_lo = jnp.zeros((half, t), F32)
                blk_hi = jnp.zeros((half, t), F32)
                idx = 0
                for j in range(sub):
                    hit = lane_h == r0 + j
                    if j < half:
                        blk_lo = jnp.where(hit & (row_h >= j), rsum[idx * half:(idx + 1) * half, :t], blk_lo)
                        blk_hi = jnp.where(hit, rsum[(idx + 1) * half:(idx + 2) * half, :t], blk_hi)
                        idx += 2
                    else:
                        blk_hi = jnp.where(hit & (row_h >= j - half), rsum[idx * half:(idx + 1) * half, :t], blk_hi)
                        idx += 1
                blk = jnp.concatenate([blk_lo, blk_hi], axis=0)
                if sc > 0:
                    g_r = gc_s[s, r0:r0 + 1, sl]
                    qs = (q[r0:r0 + sub] * jnp.exp2(gc[r0:r0 + sub] - g_r)).astype(BF16)
                    ks = jnp.concatenate(
                        [k_end[pj * sub:(pj + 1) * sub] * jnp.exp2(g_r - g_ends[pj]) for pj in range(sc)]
                        + [k_end[r0:]], axis=0).astype(BF16)
                    blk = jnp.where(lane_s < r0, _dot_nt(qs, ks), blk)
                a_rows.append(blk)
            attn = jnp.concatenate(a_rows, axis=0)
            o = o + jnp.dot(attn.astype(BF16), v16, preferred_element_type=F32)
            kd = (k * jnp.exp2(glast - gc)).astype(BF16)
            st_ref[s, h] = st * jnp.exp2(glast) + jnp.dot(vv[:, sl].T.astype(BF16), kd, preferred_element_type=F32)
            ms = jnp.mean(o * o, axis=-1, keepdims=True)
            o = o * lax.rsqrt(ms + NORM_EPS) * nw_ref[:, sl]
            o_ref[s, :, sl] = (o * _silu(gg[:, sl])).astype(o_ref.dtype)


def _hgrn(proj, lb_logits, nw, layer):
    b, l, _ = proj.shape
    t = HGRN_CHUNK
    w = BRANCH_WIDTH
    nb = min(SEQS_PER_STEP, b)
    return pl.pallas_call(
        functools.partial(_hgrn_kernel, layer=layer),
        grid=(b // nb, l // t),
        in_specs=[_seg_spec(nb, t, w, COL_BQ), _seg_spec(nb, t, w, COL_BF), _seg_spec(nb, t, w, COL_BI),
                  _seg_spec(nb, t, w, COL_BG), _const_spec(lb_logits.shape), _const_spec((1, w))],
        out_specs=pl.BlockSpec((nb, t, w), lambda i, c: (i, c, 0)),
        out_shape=jax.ShapeDtypeStruct((b, l, w), BF16),
        scratch_shapes=[pltpu.VMEM((nb, HGRN_HEADS, HGRN_HEAD_DIM, HGRN_HEAD_DIM), F32),
                        pltpu.VMEM((nb, t, w), F32), pltpu.VMEM((nb, t, w), F32),
                        pltpu.VMEM((nb, t, w), F32), pltpu.VMEM((nb, t, w), F32)],
        compiler_params=_cparams(("arbitrary", "arbitrary")),
        name="hgrn2",
    )(proj, proj, proj, proj, lb_logits, nw)


def _mlstm_kernel(qk_ref, v_ref, og_ref, sm_ref, gb_ref, nw_ref, out_ref, c_ref, m_ref):
    t = MLSTM_CHUNK
    dk = MLSTM_QK_DIM
    dv = MLSTM_V_DIM

    @pl.when(pl.program_id(1) == 0)
    def _():
        c_ref[...] = jnp.zeros_like(c_ref)
        m_ref[...] = jnp.zeros_like(m_ref)

    causal = _tril01(t)
    causal16 = causal.astype(BF16)
    eye = (lax.broadcasted_iota(jnp.int32, (LANES, LANES), 0)
           == lax.broadcasted_iota(jnp.int32, (LANES, LANES), 1)).astype(BF16)
    lane_e = lax.broadcasted_iota(jnp.int32, (t, LANES), 1)
    ones_col = jnp.where(lane_e == 0, 1.0, 0.0).astype(BF16)
    items = [(s, h) for s in range(qk_ref.shape[0]) for h in range(MLSTM_HEADS)]
    gate_cols = {}
    for s in range(qk_ref.shape[0]):
        gates = sm_ref[s] + gb_ref[...]
        bcum = _dot01_left(causal16, _log_sigmoid(gates))
        bcum_t = sum(_dot_nt(eye, p) for p in _split3(bcum))
        gates_t = sum(_dot_nt(eye, p) for p in _split3(gates))
        for h in range(MLSTM_HEADS):
            gate_cols[s, h] = (bcum[:, SM_CF + h:SM_CF + h + 1],
                               gates[:, SM_CI + h:SM_CI + h + 1],
                               bcum_t[SM_CF + h:SM_CF + h + 1, :],
                               gates_t[SM_CI + h:SM_CI + h + 1, :],
                               bcum[t - 1:t, SM_CF + h:SM_CF + h + 1])

    q16, k32, v_ext, c_st, m_st, raw, inter = {}, {}, {}, {}, {}, {}, {}
    for it in items:
        s, h = it
        q16[it] = (qk_ref[s, :, h * dk:(h + 1) * dk] * (dk ** -0.5)).astype(BF16)
        k32[it] = qk_ref[s, :, (MLSTM_HEADS + h) * dk:(MLSTM_HEADS + h + 1) * dk]
        v_ext[it] = jnp.concatenate([v_ref[s, :, h * dv:(h + 1) * dv].astype(BF16), ones_col], axis=1)
        c_st[it] = c_ref[s, h]
        m_st[it] = m_ref[s, h][:, 0:1]
        raw[it] = _dot_nt(q16[it], k32[it].astype(BF16))
        inter[it] = jnp.dot(q16[it], c_st[it].astype(BF16), preferred_element_type=F32)

    m_row, w_intra, w_inter = {}, {}, {}
    for it in items:
        bc, ig, bc_r, ig_r, _ = gate_cols[it]
        dlog = jnp.where(causal, bc - bc_r + ig_r, -jnp.inf)
        inter_log = bc + m_st[it]
        m_row[it] = jnp.maximum(inter_log, jnp.max(dlog, axis=-1, keepdims=True))
        w_intra[it] = jnp.exp(dlog - m_row[it])
        w_inter[it] = jnp.exp(inter_log - m_row[it])

    hh = {}
    for it in items:
        sc = (raw[it] * w_intra[it]).astype(BF16)
        nd = jnp.dot(sc, v_ext[it], preferred_element_type=F32) + w_inter[it] * inter[it]
        hh[it] = nd[:, :dv] / jnp.maximum(jnp.abs(nd[:, dv:dv + 1]), jnp.exp(-m_row[it]))

    for it in items:
        s, h = it
        bc, ig, _, _, b_last = gate_cols[it]
        log_w = b_last - bc + ig
        m_new = jnp.maximum(b_last + m_st[it], jnp.max(log_w, axis=0, keepdims=True))
        wk = jnp.exp(log_w - m_new)
        decay = jnp.exp(b_last + m_st[it] - m_new)
        kv = jnp.dot(k32[it].T.astype(BF16), (wk * v_ext[it].astype(F32)).astype(BF16), preferred_element_type=F32)
        c_ref[s, h] = decay * c_st[it] + kv
        m_ref[s, h] = jnp.broadcast_to(m_new, (1, LANES))

    for it in items:
        s, h = it
        ms = jnp.mean(hh[it] * hh[it], axis=-1, keepdims=True)
        hn = hh[it] * lax.rsqrt(ms + NORM_EPS) * nw_ref[:, h * dv:(h + 1) * dv]
        og = og_ref[s, :, h * dv:(h + 1) * dv]
        out_ref[s, :, h * dv:(h + 1) * dv] = (hn * _sigmoid(og)).astype(out_ref.dtype)


def _mlstm(proj, small, gate_bias, nw):
    b, l, _ = proj.shape
    t = MLSTM_CHUNK
    w = BRANCH_WIDTH
    nb = min(SEQS_PER_STEP, b)
    return pl.pallas_call(
        _mlstm_kernel,
        grid=(b // nb, l // t),
        in_specs=[_seg_spec(nb, t, w, COL_CQK), _seg_spec(nb, t, w, COL_CV), _seg_spec(nb, t, w, COL_CO),
                  pl.BlockSpec((nb, t, LANES), lambda i, c: (i, c, 0)),
                  _const_spec((1, LANES)), _const_spec((1, w))],
        out_specs=pl.BlockSpec((nb, t, w), lambda i, c: (i, c, 0)),
        out_shape=jax.ShapeDtypeStruct((b, l, w), BF16),
        scratch_shapes=[pltpu.VMEM((nb, MLSTM_HEADS, MLSTM_QK_DIM, MLSTM_V_DIM + LANES), F32),
                        pltpu.VMEM((nb, MLSTM_HEADS, 1, LANES), F32)],
        compiler_params=_cparams(("arbitrary", "arbitrary")),
        name="mlstm",
    )(proj, proj, proj, small, gate_bias, nw)


def _lru_kernel(x_ref, g_ref, cw_ref, cb_ref, wa_ref, ba_ref, wi_ref, bi_ref, ap_ref, o_ref, ext_ref, h_ref):
    t = LRU_CHUNK
    bd = LRU_BLOCK_DIM

    @pl.when(pl.program_id(1) == 0)
    def _():
        ext_ref[:, 0:SUBLANES, :] = jnp.zeros((ext_ref.shape[0], SUBLANES, ext_ref.shape[2]), F32)
        h_ref[...] = jnp.zeros_like(h_ref)

    row_in = lax.broadcasted_iota(jnp.int32, (SUBLANES, 1), 0)
    neg_sp = -LRU_C * _softplus(-ap_ref[...])
    for s in range(x_ref.shape[0]):
        xc = _causal_conv(x_ref[s], ext_ref, s, cw_ref[...], cb_ref[...])
        xc16 = xc.astype(BF16)
        r_parts, i_parts = [], []
        for n in range(LRU_BLOCKS):
            xb = xc16[:, n * bd:(n + 1) * bd]
            r_parts.append(jnp.dot(xb, wa_ref[n], preferred_element_type=F32))
            i_parts.append(jnp.dot(xb, wi_ref[n], preferred_element_type=F32))
        r = _sigmoid(jnp.concatenate(r_parts, axis=1) + ba_ref[...])
        ig = _sigmoid(jnp.concatenate(i_parts, axis=1) + bi_ref[...])
        log_a = r * neg_sp
        a = jnp.exp(log_a)
        var = -jnp.tanh(log_a) * (a * a + 1.0)
        u = xc * ig * jnp.where(var > 0.0, var * lax.rsqrt(var), 0.0)

        carry = h_ref[s, 0:1]
        groups = []
        for r0 in range(0, t, SUBLANES):
            ug = u[r0:r0 + SUBLANES]
            ag = a[r0:r0 + SUBLANES]
            d = 1
            while d < SUBLANES:
                keep = row_in >= d
                ug = ug + ag * jnp.where(keep, pltpu.roll(ug, d, 0), 0.0)
                ag = ag * jnp.where(keep, pltpu.roll(ag, d, 0), 1.0)
                d *= 2
            hg = ug + ag * carry
            carry = hg[SUBLANES - 1:SUBLANES]
            groups.append(hg)
        h_ref[s] = jnp.broadcast_to(carry, h_ref.shape[1:])
        hseq = jnp.concatenate(groups, axis=0)
        o_ref[s] = (hseq * jax.nn.gelu(g_ref[s], approximate=True)).astype(o_ref.dtype)


def _lru(proj, cw, cb, wa, ba, wi, bi, ap):
    b, l, _ = proj.shape
    t = LRU_CHUNK
    w = BRANCH_WIDTH
    nb = min(SEQS_PER_STEP, b)
    row = _const_spec((1, w))
    blk = _const_spec((LRU_BLOCKS, LRU_BLOCK_DIM, LRU_BLOCK_DIM))
    return pl.pallas_call(
        _lru_kernel,
        grid=(b // nb, l // t),
        in_specs=[_seg_spec(nb, t, w, COL_DX), _seg_spec(nb, t, w, COL_DG), _const_spec((CONV_WIDTH, w)), row,
                  blk, row, blk, row, row],
        out_specs=pl.BlockSpec((nb, t, w), lambda i, c: (i, c, 0)),
        out_shape=jax.ShapeDtypeStruct((b, l, w), BF16),
        scratch_shapes=[pltpu.VMEM((nb, SUBLANES + t, w), F32), pltpu.VMEM((nb, SUBLANES, w), F32)],
        compiler_params=_cparams(("arbitrary", "arbitrary")),
        name="rglru",
    )(proj, proj, cw, cb, wa, ba, wi, bi, ap)


def _regroup_w_in(w):
    widths = (BRANCH_WIDTH, SSM_CONV_DIM, SSM_HEADS, BRANCH_WIDTH, BRANCH_WIDTH, BRANCH_WIDTH, BRANCH_WIDTH,
              MLSTM_HEADS * MLSTM_QK_DIM, MLSTM_HEADS * MLSTM_QK_DIM, BRANCH_WIDTH, BRANCH_WIDTH,
              MLSTM_HEADS, MLSTM_HEADS, BRANCH_WIDTH, BRANCH_WIDTH, N_BRANCH * D_MODEL)
    offs = [0]
    for wd in widths:
        offs.append(offs[-1] + wd)
    seg = [w[:, offs[i]:offs[i + 1]] for i in range(len(widths))]
    (a_z, a_xbc, a_dt, b_q, b_f, b_i, b_g, c_q, c_k, c_v, c_o, c_i, c_f, d_x, d_g, gate) = seg
    main = jnp.concatenate([a_xbc, a_z, b_q, b_f, b_i, b_g, c_q, c_k, c_v, c_o, d_x, d_g, gate], axis=1)
    pad = jnp.zeros((w.shape[0], LANES - SSM_HEADS - 2 * MLSTM_HEADS), w.dtype)
    small = jnp.concatenate([a_dt, c_i, c_f, pad], axis=1)
    return main.astype(BF16), small.astype(BF16)


def _pad_lanes(v, start):
    out = jnp.zeros((1, LANES), F32)
    return out.at[0, start:start + v.shape[0]].set(v.astype(F32))


def kernel(x, p, mix_norm, w_in, ssm_conv_w, ssm_conv_b, ssm_dt_bias, ssm_a_log, ssm_d, ssm_norm, hgrn_lb_logits, hgrn_norm, mlstm_i_bias, mlstm_f_bias, mlstm_norm, lru_conv_w, lru_conv_b, lru_wa, lru_ba, lru_wi, lru_bi, lru_a_param, w_branch, w_out, mlp_norm, w_up, w_down, ple_norm, w_ple, w_ple_gate, final_norm):
    b, l, d = x.shape
    depth = w_in.shape[0]
    n = b * l
    row = lambda v: v.astype(F32).reshape(1, -1)
    e01 = (jnp.arange(LANES)[:, None] == (jnp.arange(BRANCH_WIDTH)[None, :] // SSM_HEAD_DIM)).astype(BF16)
    w_branch16, w_out16, w_up16, w_down16 = (w.astype(BF16) for w in (w_branch, w_out, w_up, w_down))
    w_ple16, w_ple_gate16, lru_wa16, lru_wi16 = (w.astype(BF16) for w in (w_ple, w_ple_gate, lru_wa, lru_wi))

    xf = x.reshape(n, d)
    for i in range(depth):
        w_main, w_small = _regroup_w_in(w_in[i])
        proj, small = _inproj(xf, row(mix_norm[i]), w_main, w_small, tm=1024, tn=1024)
        proj3 = proj.reshape(b, l, P_COLS)
        small3 = small.reshape(b, l, LANES)

        y_a = _ssd(proj3, small3, ssm_conv_w[i].astype(F32), row(ssm_conv_b[i]), _pad_lanes(ssm_dt_bias[i], SM_DT),
                   _pad_lanes(ssm_a_log[i], SM_DT), row(jnp.repeat(ssm_d[i], SSM_HEAD_DIM)), row(ssm_norm[i]), e01)
        y_b = _hgrn(proj3, hgrn_lb_logits.astype(F32), row(hgrn_norm[i]), i)
        gate_bias = _pad_lanes(jnp.concatenate([mlstm_i_bias[i], mlstm_f_bias[i]]), SM_CI)
        y_c = _mlstm(proj3, small3, gate_bias, row(mlstm_norm[i]))
        y_d = _lru(proj3, lru_conv_w[i].astype(F32), row(lru_conv_b[i]), lru_wa16[i], row(lru_ba[i]),
                   lru_wi16[i], row(lru_bi[i]), row(lru_a_param[i]))

        ys = [y.reshape(n, BRANCH_WIDTH) for y in (y_a, y_b, y_c, y_d)]
        merged = _merge(ys, proj, w_branch16[i], tm=1024, tn=512)
        xf = _mm_res(merged, w_out16[i], xf, tm=1024, tn=1024, tk=D_MODEL)

        up = _up(xf, row(mlp_norm[i]), w_up16[i], tm=1024, tn=1024)
        xf = _mm_res(up, w_down16[i], xf, tm=1024, tn=1024, tk=2048)

        xf = _ple(xf, row(ple_norm[i]), w_ple_gate16[i], p[i].reshape(n, PLE_DIM), w_ple16[i], tm=1024, tn=1024)
    return _final_norm(xf, row(final_norm), tm=512).reshape(b, l, d)
```

```python
import functools
import math

import jax
import jax.numpy as jnp
from jax import lax
from jax.experimental import pallas as pl
from jax.experimental.pallas import tpu as pltpu

F32 = jnp.float32
BF16 = jnp.bfloat16

D_MODEL = 2048
NORM_EPS = 1e-6
PLE_DIM = 256
N_BRANCH = 4
BRANCH_WIDTH = D_MODEL // 2
CONV_WIDTH = 4
SSM_HEAD_DIM = 64
SSM_HEADS = BRANCH_WIDTH // SSM_HEAD_DIM
SSM_GROUPS = 4
SSM_STATE = 128
SSM_CHUNK = 128
SSM_CONV_DIM = BRANCH_WIDTH + 2 * SSM_GROUPS * SSM_STATE
HGRN_HEAD_DIM = 128
HGRN_HEADS = BRANCH_WIDTH // HGRN_HEAD_DIM
HGRN_CHUNK = 64
HGRN_SUB = 16
MLSTM_HEADS = 4
MLSTM_QK_DIM = BRANCH_WIDTH // (2 * MLSTM_HEADS)
MLSTM_V_DIM = BRANCH_WIDTH // MLSTM_HEADS
MLSTM_CHUNK = 64
LRU_BLOCKS = 8
LRU_BLOCK_DIM = BRANCH_WIDTH // LRU_BLOCKS
LRU_C = 8.0
LRU_CHUNK = 128
D_FF = 4 * D_MODEL

LANES = 128
SUBLANES = 8
VMEM_LIMIT = 48 * 1024 * 1024
LOG2E = math.log2(math.e)
LN2 = math.log(2.0)
SEQS_PER_STEP = 2

COL_XBC = 0
COL_Z = COL_XBC + SSM_CONV_DIM
COL_BQ = COL_Z + BRANCH_WIDTH
COL_BF = COL_BQ + BRANCH_WIDTH
COL_BI = COL_BF + BRANCH_WIDTH
COL_BG = COL_BI + BRANCH_WIDTH
COL_CQK = COL_BG + BRANCH_WIDTH
COL_CV = COL_CQK + BRANCH_WIDTH
COL_CO = COL_CV + BRANCH_WIDTH
COL_DX = COL_CO + BRANCH_WIDTH
COL_DG = COL_DX + BRANCH_WIDTH
P_COLS = COL_DG + BRANCH_WIDTH
SM_DT = 0
SM_CI = SSM_HEADS
SM_CF = SSM_HEADS + MLSTM_HEADS


def _sigmoid(x):
    return 1.0 / (1.0 + jnp.exp(-x))


def _silu(x):
    return x * _sigmoid(x)


def _softplus(x):
    return jnp.maximum(x, 0.0) + LN2 * jnp.log2(1.0 + jnp.exp2(-LOG2E * jnp.abs(x)))


def _log_sigmoid(x):
    return jnp.minimum(x, 0.0) - LN2 * jnp.log2(1.0 + jnp.exp2(-LOG2E * jnp.abs(x)))


def _split3(x):
    hi = x.astype(BF16)
    r1 = x - hi.astype(F32)
    mid = r1.astype(BF16)
    lo = (r1 - mid.astype(F32)).astype(BF16)
    return hi, mid, lo


def _dot01_left(a01, x):
    return sum(jnp.dot(a01, p, preferred_element_type=F32) for p in _split3(x))


def _dot01_right(x, a01):
    return sum(jnp.dot(p, a01, preferred_element_type=F32) for p in _split3(x))


def _dot_nt(a, b):
    return lax.dot_general(a, b, (((1,), (1,)), ((), ())), preferred_element_type=F32)


def _tril01(t):
    r = lax.broadcasted_iota(jnp.int32, (t, t), 0)
    c = lax.broadcasted_iota(jnp.int32, (t, t), 1)
    return r >= c


def _causal_conv(x, ext_ref, s, cw, cb):
    t = x.shape[0]
    ext_ref[s, SUBLANES:SUBLANES + t, :] = x
    acc = cb + cw[CONV_WIDTH - 1:CONV_WIDTH] * x
    for k in range(1, CONV_WIDTH):
        acc = acc + cw[CONV_WIDTH - 1 - k:CONV_WIDTH - k] * ext_ref[s, SUBLANES - k:SUBLANES - k + t, :]
    ext_ref[s, 0:SUBLANES, :] = x[t - SUBLANES:t]
    return acc


def _cparams(sem):
    return pltpu.CompilerParams(dimension_semantics=sem, vmem_limit_bytes=VMEM_LIMIT)


def _rms_to_scratch(x_ref, nw_ref, h_ref):
    x = x_ref[...]
    ms = jnp.mean(x * x, axis=-1, keepdims=True)
    h_ref[...] = (x * lax.rsqrt(ms + NORM_EPS) * nw_ref[...]).astype(BF16)


def _inproj_kernel(x_ref, nw_ref, w_ref, ws_ref, o_ref, os_ref, h_ref):
    @pl.when(pl.program_id(1) == 0)
    def _():
        _rms_to_scratch(x_ref, nw_ref, h_ref)
        os_ref[...] = jnp.dot(h_ref[...], ws_ref[...], preferred_element_type=F32)

    o_ref[...] = jnp.dot(h_ref[...], w_ref[...], preferred_element_type=F32)


def _inproj(x, nw, w, ws, *, tm, tn):
    m, k = x.shape
    n = w.shape[1]
    tm = min(tm, m)
    return pl.pallas_call(
        _inproj_kernel,
        grid=(m // tm, n // tn),
        in_specs=[pl.BlockSpec((tm, k), lambda i, j: (i, 0)),
                  pl.BlockSpec((1, k), lambda i, j: (0, 0)),
                  pl.BlockSpec((k, tn), lambda i, j: (0, j)),
                  pl.BlockSpec((k, LANES), lambda i, j: (0, 0))],
        out_specs=[pl.BlockSpec((tm, tn), lambda i, j: (i, j)),
                   pl.BlockSpec((tm, LANES), lambda i, j: (i, 0))],
        out_shape=[jax.ShapeDtypeStruct((m, n), F32), jax.ShapeDtypeStruct((m, LANES), F32)],
        scratch_shapes=[pltpu.VMEM((tm, k), BF16)],
        compiler_params=_cparams(("arbitrary", "arbitrary")),
        name="inproj",
    )(x, nw, w, ws)


def _norm_act_kernel(x_ref, nw_ref, w_ref, o_ref, h_ref, *, act):
    @pl.when(pl.program_id(1) == 0)
    def _():
        _rms_to_scratch(x_ref, nw_ref, h_ref)

    acc = jnp.dot(h_ref[...], w_ref[...], preferred_element_type=F32)
    if act == "relu2":
        acc = jnp.square(jnp.maximum(acc, 0.0))
    else:
        acc = _sigmoid(acc)
    o_ref[...] = acc.astype(o_ref.dtype)


def _norm_act(x, nw, w, *, act, tm, tn):
    m, k = x.shape
    n = w.shape[1]
    tm = min(tm, m)
    return pl.pallas_call(
        functools.partial(_norm_act_kernel, act=act),
        grid=(m // tm, n // tn),
        in_specs=[pl.BlockSpec((tm, k), lambda i, j: (i, 0)),
                  pl.BlockSpec((1, k), lambda i, j: (0, 0)),
                  pl.BlockSpec((k, tn), lambda i, j: (0, j))],
        out_specs=pl.BlockSpec((tm, tn), lambda i, j: (i, j)),
        out_shape=jax.ShapeDtypeStruct((m, n), BF16),
        scratch_shapes=[pltpu.VMEM((tm, k), BF16)],
        compiler_params=_cparams(("arbitrary", "arbitrary")),
        name="mlp_up" if act == "relu2" else "branch_gates",
    )(x, nw, w)


def _ple_kernel(x_ref, nw_ref, wg_ref, p_ref, wp_ref, o_ref, h_ref, *, tn):
    j = pl.program_id(1)

    @pl.when(j == 0)
    def _():
        _rms_to_scratch(x_ref, nw_ref, h_ref)

    gate = jnp.dot(h_ref[...], wg_ref[...], preferred_element_type=F32)
    emb = jnp.dot(p_ref[...].astype(BF16), wp_ref[...], preferred_element_type=F32)
    xc = x_ref[:, pl.ds(pl.multiple_of(j * tn, tn), tn)]
    o_ref[...] = xc + emb * _sigmoid(gate)


def _ple(x, nw, wg, p, wp, *, tm, tn):
    m, k = x.shape
    tm = min(tm, m)
    kp = p.shape[1]
    return pl.pallas_call(
        functools.partial(_ple_kernel, tn=tn),
        grid=(m // tm, k // tn),
        in_specs=[pl.BlockSpec((tm, k), lambda i, j: (i, 0)),
                  pl.BlockSpec((1, k), lambda i, j: (0, 0)),
                  pl.BlockSpec((k, tn), lambda i, j: (0, j)),
                  pl.BlockSpec((tm, kp), lambda i, j: (i, 0)),
                  pl.BlockSpec((kp, tn), lambda i, j: (0, j))],
        out_specs=pl.BlockSpec((tm, tn), lambda i, j: (i, j)),
        out_shape=jax.ShapeDtypeStruct((m, k), F32),
        scratch_shapes=[pltpu.VMEM((tm, k), BF16)],
        compiler_params=_cparams(("arbitrary", "arbitrary")),
        name="ple",
    )(x, nw, wg, p, wp)


def _mm_res_kernel(a_ref, w_ref, r_ref, o_ref, acc_ref):
    kk = pl.program_id(2)

    @pl.when(kk == 0)
    def _():
        acc_ref[...] = r_ref[...]

    acc_ref[...] += jnp.dot(a_ref[...], w_ref[...], preferred_element_type=F32)

    @pl.when(kk == pl.num_programs(2) - 1)
    def _():
        o_ref[...] = acc_ref[...]


def _mm_res(a, w, r, *, tm, tn, tk):
    m, k = a.shape
    n = w.shape[1]
    tm = min(tm, m)
    return pl.pallas_call(
        _mm_res_kernel,
        grid=(m // tm, n // tn, k // tk),
        in_specs=[pl.BlockSpec((tm, tk), lambda i, j, q: (i, q)),
                  pl.BlockSpec((tk, tn), lambda i, j, q: (q, j)),
                  pl.BlockSpec((tm, tn), lambda i, j, q: (i, j))],
        out_specs=pl.BlockSpec((tm, tn), lambda i, j, q: (i, j)),
        out_shape=jax.ShapeDtypeStruct((m, n), F32),
        scratch_shapes=[pltpu.VMEM((tm, tn), F32)],
        compiler_params=_cparams(("arbitrary", "arbitrary", "arbitrary")),
        name="mm_res",
    )(a, w, r)


def _merge_kernel(ya_ref, yb_ref, yc_ref, yd_ref, ga_ref, gb_ref, gc_ref, gd_ref, wb_ref, o_ref):
    acc = None
    for br, (y_ref, g_ref) in enumerate(((ya_ref, ga_ref), (yb_ref, gb_ref), (yc_ref, gc_ref), (yd_ref, gd_ref))):
        t = g_ref[...].astype(F32) * jnp.dot(y_ref[...], wb_ref[br], preferred_element_type=F32)
        acc = t if acc is None else acc + t
    o_ref[...] = acc.astype(o_ref.dtype)


def _merge(ys, gates2d, wb, *, tm, tn):
    m = gates2d.shape[0]
    tm = min(tm, m)
    y_spec = pl.BlockSpec((tm, BRANCH_WIDTH), lambda i, j: (i, 0))
    gate_specs = [pl.BlockSpec((tm, tn), functools.partial(
        lambda i, j, off: (i, off + j), off=(br * D_MODEL) // tn)) for br in range(N_BRANCH)]
    return pl.pallas_call(
        _merge_kernel,
        grid=(m // tm, D_MODEL // tn),
        in_specs=[y_spec] * N_BRANCH + gate_specs + [pl.BlockSpec((N_BRANCH, BRANCH_WIDTH, tn), lambda i, j: (0, 0, j))],
        out_specs=pl.BlockSpec((tm, tn), lambda i, j: (i, j)),
        out_shape=jax.ShapeDtypeStruct((m, D_MODEL), BF16),
        compiler_params=_cparams(("arbitrary", "arbitrary")),
        name="merge",
    )(*ys, gates2d, gates2d, gates2d, gates2d, wb)


def _final_norm_kernel(x_ref, nw_ref, o_ref):
    x = x_ref[...]
    ms = jnp.mean(x * x, axis=-1, keepdims=True)
    o_ref[...] = x * lax.rsqrt(ms + NORM_EPS) * nw_ref[...]


def _final_norm(x, nw, *, tm):
    m, k = x.shape
    tm = min(tm, m)
    return pl.pallas_call(
        _final_norm_kernel,
        grid=(m // tm,),
        in_specs=[pl.BlockSpec((tm, k), lambda i: (i, 0)), pl.BlockSpec((1, k), lambda i: (0, 0))],
        out_specs=pl.BlockSpec((tm, k), lambda i: (i, 0)),
        out_shape=jax.ShapeDtypeStruct((m, k), F32),
        compiler_params=_cparams(("arbitrary",)),
        name="final_norm",
    )(x, nw)


def _seg_spec(nb, t, width, col):
    return pl.BlockSpec((nb, t, width), functools.partial(lambda i, c, cb: (i, c, cb), cb=col // width))


def _const_spec(shape):
    return pl.BlockSpec(shape, lambda i, c: (0,) * len(shape))


def _ssd_kernel(xbc_ref, z_ref, sm_ref, cw_ref, cb_ref, dtb_ref, alog_ref, dsk_ref, nw_ref, e_ref,
                o_ref, ext_ref, state_ref):
    t = SSM_CHUNK
    gw = BRANCH_WIDTH // SSM_GROUPS
    heads_per_group = SSM_HEADS // SSM_GROUPS

    @pl.when(pl.program_id(1) == 0)
    def _():
        ext_ref[:, 0:SUBLANES, :] = jnp.zeros((ext_ref.shape[0], SUBLANES, ext_ref.shape[2]), F32)
        state_ref[...] = jnp.zeros_like(state_ref)

    causal = _tril01(t)
    causal16 = causal.astype(BF16)
    e01 = e_ref[...]
    lane_hi = lax.broadcasted_iota(jnp.int32, (1, LANES), 1) >= SSM_HEAD_DIM
    neg_a = -jnp.exp(alog_ref[...])
    for s in range(xbc_ref.shape[0]):
        u = _silu(_causal_conv(xbc_ref[s], ext_ref, s, cw_ref[...], cb_ref[...]))
        xs = u[:, :BRANCH_WIDTH]
        bm = u[:, BRANCH_WIDTH:BRANCH_WIDTH + SSM_GROUPS * SSM_STATE]
        cm = u[:, BRANCH_WIDTH + SSM_GROUPS * SSM_STATE:]

        dt = _softplus(sm_ref[s] + dtb_ref[...])
        cs = _dot01_left(causal16, dt * neg_a)
        cs_t = cs.T
        dt_t = dt.T
        tot = cs[t - 1:t]
        w_exp = _dot01_right(jnp.exp(tot - cs) * dt, e01)
        ecs_exp = _dot01_right(jnp.exp(cs), e01)
        etot_exp = _dot01_right(jnp.broadcast_to(jnp.exp(tot), (SUBLANES, LANES)), e01)[0:1]

        groups = range(SSM_GROUPS)
        gsls = [slice(g * gw, (g + 1) * gw) for g in groups]
        heads_per_pair = LANES // SSM_HEAD_DIM

        cb_mat, y_off = {}, {}
        for g in groups:
            bg = bm[:, g * SSM_STATE:(g + 1) * SSM_STATE]
            cg16 = cm[:, g * SSM_STATE:(g + 1) * SSM_STATE].astype(BF16)
            cb_mat[g] = _dot_nt(cg16, bg.astype(BF16))
            st = state_ref[s, g]
            y_off[g] = jnp.dot(cg16, st.astype(BF16), preferred_element_type=F32) * ecs_exp[:, gsls[g]]
            xw = (xs[:, gsls[g]] * w_exp[:, gsls[g]]).astype(BF16)
            state_ref[s, g] = st * etot_exp[:, gsls[g]] + jnp.dot(bg.T.astype(BF16), xw, preferred_element_type=F32)

        mix = {}
        for h in range(SSM_HEADS):
            seg = jnp.minimum(cs[:, h:h + 1] - cs_t[h:h + 1, :], 0.0)
            lmat = jnp.where(causal, jnp.exp(seg), 0.0)
            mix[h] = (cb_mat[h // heads_per_group] * lmat * dt_t[h:h + 1, :]).astype(BF16)

        y_diag = {}
        for pr in range(SSM_HEADS // heads_per_pair):
            xpair = xs[:, pr * LANES:(pr + 1) * LANES]
            acc = None
            for hh in range(heads_per_pair):
                xm = jnp.where(lane_hi if hh else jnp.logical_not(lane_hi), xpair, 0.0).astype(BF16)
                part = jnp.dot(mix[pr * heads_per_pair + hh], xm, preferred_element_type=F32)
                acc = part if acc is None else acc + part
            y_diag[pr] = acc

        z = z_ref[s]
        pairs_per_group = gw // LANES
        for g in groups:
            gsl = gsls[g]
            y = y_off[g] + jnp.concatenate([y_diag[g * pairs_per_group + pr] for pr in range(pairs_per_group)], axis=1)
            y = (y + dsk_ref[:, gsl] * xs[:, gsl]) * _silu(z[:, gsl])
            ms = jnp.mean(y * y, axis=-1, keepdims=True)
            o_ref[s, :, gsl] = (y * lax.rsqrt(ms + NORM_EPS) * nw_ref[:, gsl]).astype(o_ref.dtype)


def _ssd(proj, small, cw, cb, dtb, alog, dsk, nw, e01):
    b, l, _ = proj.shape
    t = SSM_CHUNK
    nb = min(SEQS_PER_STEP, b)
    return pl.pallas_call(
        _ssd_kernel,
        grid=(b // nb, l // t),
        in_specs=[_seg_spec(nb, t, SSM_CONV_DIM, COL_XBC), _seg_spec(nb, t, BRANCH_WIDTH, COL_Z),
                  pl.BlockSpec((nb, t, LANES), lambda i, c: (i, c, 0)),
                  _const_spec((CONV_WIDTH, SSM_CONV_DIM)), _const_spec((1, SSM_CONV_DIM)),
                  _const_spec((1, LANES)), _const_spec((1, LANES)),
                  _const_spec((1, BRANCH_WIDTH)), _const_spec((1, BRANCH_WIDTH)), _const_spec((LANES, BRANCH_WIDTH))],
        out_specs=pl.BlockSpec((nb, t, BRANCH_WIDTH), lambda i, c: (i, c, 0)),
        out_shape=jax.ShapeDtypeStruct((b, l, BRANCH_WIDTH), BF16),
        scratch_shapes=[pltpu.VMEM((nb, SUBLANES + t, SSM_CONV_DIM), F32),
                        pltpu.VMEM((nb, SSM_GROUPS, SSM_STATE, BRANCH_WIDTH // SSM_GROUPS), F32)],
        compiler_params=_cparams(("arbitrary", "arbitrary")),
        name="ssd",
    )(proj, proj, small, cw, cb, dtb, alog, dsk, nw, e01)


def _hgrn_kernel(q_ref, f_ref, i_ref, g_ref, lbl_ref, nw_ref, o_ref, st_ref, k_s, gc_s, gk_s, lk_s, *, layer):
    t = HGRN_CHUNK
    sub = HGRN_SUB
    half = SUBLANES
    hd = HGRN_HEAD_DIM

    @pl.when(pl.program_id(1) == 0)
    def _():
        st_ref[...] = jnp.zeros_like(st_ref)

    logits = lbl_ref[...]
    ex = jnp.exp(logits - jnp.max(logits, axis=0, keepdims=True))
    sm = ex / jnp.sum(ex, axis=0, keepdims=True)
    lb = jnp.zeros((1, BRANCH_WIDTH), F32)
    for r in range(1, layer + 1):
        lb = lb + sm[r:r + 1]
    la = jnp.log(lb)
    l1 = jnp.log1p(-lb)

    tril16 = _tril01(t).astype(BF16)
    ones16 = jnp.ones((hd, hd), BF16)
    lane_h = lax.broadcasted_iota(jnp.int32, (half, t), 1)
    row_h = lax.broadcasted_iota(jnp.int32, (half, t), 0)
    lane_s = lax.broadcasted_iota(jnp.int32, (sub, t), 1)
    for s in range(q_ref.shape[0]):
        fp = f_ref[s]
        ls = _log_sigmoid(fp)
        lbb = l1 + ls
        logf = jnp.maximum(la, lbb) + LN2 * jnp.log2(1.0 + jnp.exp2(-LOG2E * jnp.abs(la - lbb)))
        k_s[s] = (1.0 - lb) * _sigmoid(-fp)
        lk = (l1 + (ls - fp)) * LOG2E
        gc2 = _dot01_left(tril16, logf) * LOG2E
        gc_s[s] = gc2
        gk_s[s] = gc2 - lk
        lk_s[s] = lk
        qq = _silu(q_ref[s]) * (hd ** -0.5)
        vv = i_ref[s]
        gg = g_ref[s]
        heads = range(HGRN_HEADS)
        subs = range(t // sub)
        sls = [slice(h * hd, (h + 1) * hd) for h in heads]

        o_inter, k_end, g_ends = {}, {}, {}
        for h in heads:
            sl = sls[h]
            k = k_s[s, :, sl]
            gc = gc_s[s, :, sl]
            glast = gc_s[s, t - 1:t, sl]
            st = st_ref[s, h]
            o_inter[h] = _dot_nt((qq[:, sl] * jnp.exp2(gc)).astype(BF16), st.astype(BF16))
            g_ends[h] = [gc_s[s, (sc + 1) * sub - 1:(sc + 1) * sub, sl] for sc in subs]
            g_end_rows = jnp.concatenate([jnp.broadcast_to(g, (sub, hd)) for g in g_ends[h]], axis=0)
            k_end[h] = k * jnp.exp2(g_end_rows - gc)
            kd = (k * jnp.exp2(glast - gc)).astype(BF16)
            st_ref[s, h] = st * jnp.exp2(glast) + jnp.dot(vv[:, sl].T.astype(BF16), kd, preferred_element_type=F32)

        rsum = {}
        for h in heads:
            sl = sls[h]
            for sc in subs:
                r0 = sc * sub
                q_lo, q_hi = qq[r0:r0 + half, sl], qq[r0 + half:r0 + sub, sl]
                g_lo, g_hi = gc_s[s, r0:r0 + half, sl], gc_s[s, r0 + half:r0 + sub, sl]
                prods = []
                for j in range(sub):
                    gk_j = gk_s[s, r0 + j:r0 + j + 1, sl]
                    lk_j = lk_s[s, r0 + j:r0 + j + 1, sl]
                    if j < half:
                        prods.append(q_lo * jnp.exp2(jnp.minimum(g_lo - gk_j, lk_j)))
                        prods.append(q_hi * jnp.exp2(g_hi - gk_j))
                    else:
                        prods.append(q_hi * jnp.exp2(jnp.minimum(g_hi - gk_j, lk_j)))
                rsum[h, sc] = jnp.dot(jnp.concatenate(prods, axis=0).astype(BF16), ones16, preferred_element_type=F32)

        off = {}
        for h in heads:
            sl = sls[h]
            for sc in subs[1:]:
                r0 = sc * sub
                g_r = gc_s[s, r0:r0 + 1, sl]
                qs = (qq[r0:r0 + sub, sl] * jnp.exp2(gc_s[s, r0:r0 + sub, sl] - g_r)).astype(BF16)
                ks = jnp.concatenate(
                    [k_end[h][pj * sub:(pj + 1) * sub] * jnp.exp2(g_r - g_ends[h][pj]) for pj in range(sc)]
                    + [k_end[h][r0:]], axis=0).astype(BF16)
                off[h, sc] = _dot_nt(qs, ks)

        for h in heads:
            sl = sls[h]
            a_rows = []
            for sc in subs:
                r0 = sc * sub
                blk_lo = jnp.zeros((half, t), F32)
                blk_hi = jnp.zeros((half, t), F32)
                idx = 0
                for j in range(sub):
                    hit = lane_h == r0 + j
                    if j < half:
                        blk_lo = jnp.where(hit & (row_h >= j), rsum[h, sc][idx * half:(idx + 1) * half, :t], blk_lo)
                        blk_hi = jnp.where(hit, rsum[h, sc][(idx + 1) * half:(idx + 2) * half, :t], blk_hi)
                        idx += 2
                    else:
                        blk_hi = jnp.where(hit & (row_h >= j - half), rsum[h, sc][idx * half:(idx + 1) * half, :t], blk_hi)
                        idx += 1
                blk = jnp.concatenate([blk_lo, blk_hi], axis=0)
                if sc > 0:
                    blk = jnp.where(lane_s < r0, off[h, sc], blk)
                a_rows.append(blk)
            attn = jnp.concatenate(a_rows, axis=0)
            o = o_inter[h] + jnp.dot(attn.astype(BF16), vv[:, sl].astype(BF16), preferred_element_type=F32)
            ms = jnp.mean(o * o, axis=-1, keepdims=True)
            o = o * lax.rsqrt(ms + NORM_EPS) * nw_ref[:, sl]
            o_ref[s, :, sl] = (o * _silu(gg[:, sl])).astype(o_ref.dtype)


def _hgrn(proj, lb_logits, nw, layer):
    b, l, _ = proj.shape
    t = HGRN_CHUNK
    w = BRANCH_WIDTH
    nb = min(SEQS_PER_STEP, b)
    return pl.pallas_call(
        functools.partial(_hgrn_kernel, layer=layer),
        grid=(b // nb, l // t),
        in_specs=[_seg_spec(nb, t, w, COL_BQ), _seg_spec(nb, t, w, COL_BF), _seg_spec(nb, t, w, COL_BI),
                  _seg_spec(nb, t, w, COL_BG), _const_spec(lb_logits.shape), _const_spec((1, w))],
        out_specs=pl.BlockSpec((nb, t, w), lambda i, c: (i, c, 0)),
        out_shape=jax.ShapeDtypeStruct((b, l, w), BF16),
        scratch_shapes=[pltpu.VMEM((nb, HGRN_HEADS, HGRN_HEAD_DIM, HGRN_HEAD_DIM), F32),
                        pltpu.VMEM((nb, t, w), F32), pltpu.VMEM((nb, t, w), F32),
                        pltpu.VMEM((nb, t, w), F32), pltpu.VMEM((nb, t, w), F32)],
        compiler_params=_cparams(("arbitrary", "arbitrary")),
        name="hgrn2",
    )(proj, proj, proj, proj, lb_logits, nw)


def _mlstm_kernel(qk_ref, v_ref, og_ref, sm_ref, gb_ref, nw_ref, out_ref, c_ref, m_ref):
    t = MLSTM_CHUNK
    dk = MLSTM_QK_DIM
    dv = MLSTM_V_DIM

    @pl.when(pl.program_id(1) == 0)
    def _():
        c_ref[...] = jnp.zeros_like(c_ref)
        m_ref[...] = jnp.zeros_like(m_ref)

    causal = _tril01(t)
    causal16 = causal.astype(BF16)
    eye = (lax.broadcasted_iota(jnp.int32, (LANES, LANES), 0)
           == lax.broadcasted_iota(jnp.int32, (LANES, LANES), 1)).astype(BF16)
    lane_e = lax.broadcasted_iota(jnp.int32, (t, LANES), 1)
    ones_col = jnp.where(lane_e == 0, 1.0, 0.0).astype(BF16)
    items = [(s, h) for s in range(qk_ref.shape[0]) for h in range(MLSTM_HEADS)]
    gate_cols = {}
    for s in range(qk_ref.shape[0]):
        gates = sm_ref[s] + gb_ref[...]
        bcum = _dot01_left(causal16, _log_sigmoid(gates))
        bcum_t = sum(_dot_nt(eye, p) for p in _split3(bcum))
        gates_t = sum(_dot_nt(eye, p) for p in _split3(gates))
        for h in range(MLSTM_HEADS):
            gate_cols[s, h] = (bcum[:, SM_CF + h:SM_CF + h + 1],
                               gates[:, SM_CI + h:SM_CI + h + 1],
                               bcum_t[SM_CF + h:SM_CF + h + 1, :],
                               gates_t[SM_CI + h:SM_CI + h + 1, :],
                               bcum[t - 1:t, SM_CF + h:SM_CF + h + 1])

    q16, k32, v_ext, c_st, m_st, raw, inter = {}, {}, {}, {}, {}, {}, {}
    for it in items:
        s, h = it
        q16[it] = (qk_ref[s, :, h * dk:(h + 1) * dk] * (dk ** -0.5)).astype(BF16)
        k32[it] = qk_ref[s, :, (MLSTM_HEADS + h) * dk:(MLSTM_HEADS + h + 1) * dk]
        v_ext[it] = jnp.concatenate([v_ref[s, :, h * dv:(h + 1) * dv].astype(BF16), ones_col], axis=1)
        c_st[it] = c_ref[s, h]
        m_st[it] = m_ref[s, h][:, 0:1]
        raw[it] = _dot_nt(q16[it], k32[it].astype(BF16))
        inter[it] = jnp.dot(q16[it], c_st[it].astype(BF16), preferred_element_type=F32)

    m_row, w_intra, w_inter = {}, {}, {}
    for it in items:
        bc, ig, bc_r, ig_r, _ = gate_cols[it]
        dlog = jnp.where(causal, bc - bc_r + ig_r, -jnp.inf)
        inter_log = bc + m_st[it]
        m_row[it] = jnp.maximum(inter_log, jnp.max(dlog, axis=-1, keepdims=True))
        w_intra[it] = jnp.exp(dlog - m_row[it])
        w_inter[it] = jnp.exp(inter_log - m_row[it])

    hh = {}
    for it in items:
        sc = (raw[it] * w_intra[it]).astype(BF16)
        nd = jnp.dot(sc, v_ext[it], preferred_element_type=F32) + w_inter[it] * inter[it]
        hh[it] = nd[:, :dv] / jnp.maximum(jnp.abs(nd[:, dv:dv + 1]), jnp.exp(-m_row[it]))

    for it in items:
        s, h = it
        bc, ig, _, _, b_last = gate_cols[it]
        log_w = b_last - bc + ig
        m_new = jnp.maximum(b_last + m_st[it], jnp.max(log_w, axis=0, keepdims=True))
        wk = jnp.exp(log_w - m_new)
        decay = jnp.exp(b_last + m_st[it] - m_new)
        kv = jnp.dot(k32[it].T.astype(BF16), (wk * v_ext[it].astype(F32)).astype(BF16), preferred_element_type=F32)
        c_ref[s, h] = decay * c_st[it] + kv
        m_ref[s, h] = jnp.broadcast_to(m_new, (1, LANES))

    for it in items:
        s, h = it
        ms = jnp.mean(hh[it] * hh[it], axis=-1, keepdims=True)
        hn = hh[it] * lax.rsqrt(ms + NORM_EPS) * nw_ref[:, h * dv:(h + 1) * dv]
        og = og_ref[s, :, h * dv:(h + 1) * dv]
        out_ref[s, :, h * dv:(h + 1) * dv] = (hn * _sigmoid(og)).astype(out_ref.dtype)


def _mlstm(proj, small, gate_bias, nw):
    b, l, _ = proj.shape
    t = MLSTM_CHUNK
    w = BRANCH_WIDTH
    nb = min(SEQS_PER_STEP, b)
    return pl.pallas_call(
        _mlstm_kernel,
        grid=(b // nb, l // t),
        in_specs=[_seg_spec(nb, t, w, COL_CQK), _seg_spec(nb, t, w, COL_CV), _seg_spec(nb, t, w, COL_CO),
                  pl.BlockSpec((nb, t, LANES), lambda i, c: (i, c, 0)),
                  _const_spec((1, LANES)), _const_spec((1, w))],
        out_specs=pl.BlockSpec((nb, t, w), lambda i, c: (i, c, 0)),
        out_shape=jax.ShapeDtypeStruct((b, l, w), BF16),
        scratch_shapes=[pltpu.VMEM((nb, MLSTM_HEADS, MLSTM_QK_DIM, MLSTM_V_DIM + LANES), F32),
                        pltpu.VMEM((nb, MLSTM_HEADS, 1, LANES), F32)],
        compiler_params=_cparams(("arbitrary", "arbitrary")),
        name="mlstm",
    )(proj, proj, proj, small, gate_bias, nw)


def _lru_kernel(x_ref, g_ref, cw_ref, cb_ref, wa_ref, ba_ref, wi_ref, bi_ref, ap_ref, o_ref, ext_ref, h_ref):
    t = LRU_CHUNK
    bd = LRU_BLOCK_DIM

    @pl.when(pl.program_id(1) == 0)
    def _():
        ext_ref[:, 0:SUBLANES, :] = jnp.zeros((ext_ref.shape[0], SUBLANES, ext_ref.shape[2]), F32)
        h_ref[...] = jnp.zeros_like(h_ref)

    row_in = lax.broadcasted_iota(jnp.int32, (SUBLANES, 1), 0)
    neg_sp = -LRU_C * _softplus(-ap_ref[...])
    for s in range(x_ref.shape[0]):
        xc = _causal_conv(x_ref[s], ext_ref, s, cw_ref[...], cb_ref[...])
        xc16 = xc.astype(BF16)
        r_parts, i_parts = [], []
        for n in range(LRU_BLOCKS):
            xb = xc16[:, n * bd:(n + 1) * bd]
            r_parts.append(jnp.dot(xb, wa_ref[n], preferred_element_type=F32))
            i_parts.append(jnp.dot(xb, wi_ref[n], preferred_element_type=F32))
        r = _sigmoid(jnp.concatenate(r_parts, axis=1) + ba_ref[...])
        ig = _sigmoid(jnp.concatenate(i_parts, axis=1) + bi_ref[...])
        log_a = r * neg_sp
        a = jnp.exp(log_a)
        var = -jnp.tanh(log_a) * (a * a + 1.0)
        u = xc * ig * jnp.where(var > 0.0, var * lax.rsqrt(var), 0.0)

        carry = h_ref[s, 0:1]
        groups = []
        for r0 in range(0, t, SUBLANES):
            ug = u[r0:r0 + SUBLANES]
            ag = a[r0:r0 + SUBLANES]
            d = 1
            while d < SUBLANES:
                keep = row_in >= d
                ug = ug + ag * jnp.where(keep, pltpu.roll(ug, d, 0), 0.0)
                ag = ag * jnp.where(keep, pltpu.roll(ag, d, 0), 1.0)
                d *= 2
            hg = ug + ag * carry
            carry = hg[SUBLANES - 1:SUBLANES]
            groups.append(hg)
        h_ref[s] = jnp.broadcast_to(carry, h_ref.shape[1:])
        hseq = jnp.concatenate(groups, axis=0)
        o_ref[s] = (hseq * jax.nn.gelu(g_ref[s], approximate=True)).astype(o_ref.dtype)


def _lru(proj, cw, cb, wa, ba, wi, bi, ap):
    b, l, _ = proj.shape
    t = LRU_CHUNK
    w = BRANCH_WIDTH
    nb = min(SEQS_PER_STEP, b)
    row = _const_spec((1, w))
    blk = _const_spec((LRU_BLOCKS, LRU_BLOCK_DIM, LRU_BLOCK_DIM))
    return pl.pallas_call(
        _lru_kernel,
        grid=(b // nb, l // t),
        in_specs=[_seg_spec(nb, t, w, COL_DX), _seg_spec(nb, t, w, COL_DG), _const_spec((CONV_WIDTH, w)), row,
                  blk, row, blk, row, row],
        out_specs=pl.BlockSpec((nb, t, w), lambda i, c: (i, c, 0)),
        out_shape=jax.ShapeDtypeStruct((b, l, w), BF16),
        scratch_shapes=[pltpu.VMEM((nb, SUBLANES + t, w), F32), pltpu.VMEM((nb, SUBLANES, w), F32)],
        compiler_params=_cparams(("arbitrary", "arbitrary")),
        name="rglru",
    )(proj, proj, cw, cb, wa, ba, wi, bi, ap)


def _regroup_w_in(w):
    widths = (BRANCH_WIDTH, SSM_CONV_DIM, SSM_HEADS, BRANCH_WIDTH, BRANCH_WIDTH, BRANCH_WIDTH, BRANCH_WIDTH,
              MLSTM_HEADS * MLSTM_QK_DIM, MLSTM_HEADS * MLSTM_QK_DIM, BRANCH_WIDTH, BRANCH_WIDTH,
              MLSTM_HEADS, MLSTM_HEADS, BRANCH_WIDTH, BRANCH_WIDTH, N_BRANCH * D_MODEL)
    offs = [0]
    for wd in widths:
        offs.append(offs[-1] + wd)
    seg = [w[:, offs[i]:offs[i + 1]] for i in range(len(widths))]
    (a_z, a_xbc, a_dt, b_q, b_f, b_i, b_g, c_q, c_k, c_v, c_o, c_i, c_f, d_x, d_g, gate) = seg
    main = jnp.concatenate([a_xbc, a_z, b_q, b_f, b_i, b_g, c_q, c_k, c_v, c_o, d_x, d_g], axis=1)
    pad = jnp.zeros((w.shape[0], LANES - SSM_HEADS - 2 * MLSTM_HEADS), w.dtype)
    small = jnp.concatenate([a_dt, c_i, c_f, pad], axis=1)
    return main.astype(BF16), small.astype(BF16), gate.astype(BF16)


def _pad_lanes(v, start):
    out = jnp.zeros((1, LANES), F32)
    return out.at[0, start:start + v.shape[0]].set(v.astype(F32))


def kernel(x, p, mix_norm, w_in, ssm_conv_w, ssm_conv_b, ssm_dt_bias, ssm_a_log, ssm_d, ssm_norm, hgrn_lb_logits, hgrn_norm, mlstm_i_bias, mlstm_f_bias, mlstm_norm, lru_conv_w, lru_conv_b, lru_wa, lru_ba, lru_wi, lru_bi, lru_a_param, w_branch, w_out, mlp_norm, w_up, w_down, ple_norm, w_ple, w_ple_gate, final_norm):
    b, l, d = x.shape
    depth = w_in.shape[0]
    n = b * l
    row = lambda v: v.astype(F32).reshape(1, -1)
    e01 = (jnp.arange(LANES)[:, None] == (jnp.arange(BRANCH_WIDTH)[None, :] // SSM_HEAD_DIM)).astype(BF16)
    w_branch16, w_out16, w_up16, w_down16 = (w.astype(BF16) for w in (w_branch, w_out, w_up, w_down))
    w_ple16, w_ple_gate16, lru_wa16, lru_wi16 = (w.astype(BF16) for w in (w_ple, w_ple_gate, lru_wa, lru_wi))

    xf = x.reshape(n, d)
    for i in range(depth):
        w_main, w_small, w_gate = _regroup_w_in(w_in[i])
        proj, small = _inproj(xf, row(mix_norm[i]), w_main, w_small, tm=1024, tn=1024)
        gates = _norm_act(xf, row(mix_norm[i]), w_gate, act="sigmoid", tm=1024, tn=1024)
        proj3 = proj.reshape(b, l, P_COLS)
        small3 = small.reshape(b, l, LANES)

        y_a = _ssd(proj3, small3, ssm_conv_w[i].astype(F32), row(ssm_conv_b[i]), _pad_lanes(ssm_dt_bias[i], SM_DT),
                   _pad_lanes(ssm_a_log[i], SM_DT), row(jnp.repeat(ssm_d[i], SSM_HEAD_DIM)), row(ssm_norm[i]), e01)
        y_b = _hgrn(proj3, hgrn_lb_logits.astype(F32), row(hgrn_norm[i]), i)
        gate_bias = _pad_lanes(jnp.concatenate([mlstm_i_bias[i], mlstm_f_bias[i]]), SM_CI)
        y_c = _mlstm(proj3, small3, gate_bias, row(mlstm_norm[i]))
        y_d = _lru(proj3, lru_conv_w[i].astype(F32), row(lru_conv_b[i]), lru_wa16[i], row(lru_ba[i]),
                   lru_wi16[i], row(lru_bi[i]), row(lru_a_param[i]))

        ys = [y.reshape(n, BRANCH_WIDTH) for y in (y_a, y_b, y_c, y_d)]
        merged = _merge(ys, gates, w_branch16[i], tm=1024, tn=512)
        xf = _mm_res(merged, w_out16[i], xf, tm=1024, tn=1024, tk=D_MODEL)

        up = _norm_act(xf, row(mlp_norm[i]), w_up16[i], act="relu2", tm=1024, tn=1024)
        xf = _mm_res(up, w_down16[i], xf, tm=1024, tn=1024, tk=2048)

        xf = _ple(xf, row(ple_norm[i]), w_ple_gate16[i], p[i].reshape(n, PLE_DIM), w_ple16[i], tm=1024, tn=1024)
    return _final_norm(xf, row(final_norm), tm=512).reshape(b, l, d)
```

```python
import functools
import math

import jax
import jax.numpy as jnp
from jax import lax
from jax.experimental import pallas as pl
from jax.experimental.pallas import tpu as pltpu

F32 = jnp.float32
BF16 = jnp.bfloat16

D_MODEL = 2048
NORM_EPS = 1e-6
PLE_DIM = 256
N_BRANCH = 4
BRANCH_WIDTH = D_MODEL // 2
CONV_WIDTH = 4
SSM_HEAD_DIM = 64
SSM_HEADS = BRANCH_WIDTH // SSM_HEAD_DIM
SSM_GROUPS = 4
SSM_STATE = 128
SSM_CHUNK = 128
SSM_CONV_DIM = BRANCH_WIDTH + 2 * SSM_GROUPS * SSM_STATE
HGRN_HEAD_DIM = 128
HGRN_HEADS = BRANCH_WIDTH // HGRN_HEAD_DIM
HGRN_CHUNK = 64
HGRN_SUB = 16
MLSTM_HEADS = 4
MLSTM_QK_DIM = BRANCH_WIDTH // (2 * MLSTM_HEADS)
MLSTM_V_DIM = BRANCH_WIDTH // MLSTM_HEADS
MLSTM_CHUNK = 64
LRU_BLOCKS = 8
LRU_BLOCK_DIM = BRANCH_WIDTH // LRU_BLOCKS
LRU_C = 8.0
LRU_CHUNK = 128
D_FF = 4 * D_MODEL

LANES = 128
SUBLANES = 8
VMEM_LIMIT = 48 * 1024 * 1024
LOG2E = math.log2(math.e)
LN2 = math.log(2.0)
SEQS_PER_STEP = 2
MLSTM_SEQS_PER_STEP = 4

COL_XBC = 0
COL_Z = COL_XBC + SSM_CONV_DIM
COL_BQ = COL_Z + BRANCH_WIDTH
COL_BF = COL_BQ + BRANCH_WIDTH
COL_BI = COL_BF + BRANCH_WIDTH
COL_BG = COL_BI + BRANCH_WIDTH
COL_CQK = COL_BG + BRANCH_WIDTH
COL_CV = COL_CQK + BRANCH_WIDTH
COL_CO = COL_CV + BRANCH_WIDTH
COL_DX = COL_CO + BRANCH_WIDTH
COL_DG = COL_DX + BRANCH_WIDTH
COL_GATE = COL_DG + BRANCH_WIDTH
P_COLS = COL_GATE + N_BRANCH * D_MODEL
SM_DT = 0
SM_CI = SSM_HEADS
SM_CF = SSM_HEADS + MLSTM_HEADS


def _sigmoid(x):
    return 1.0 / (1.0 + jnp.exp(-x))


def _silu(x):
    return x * _sigmoid(x)


def _softplus(x):
    return jnp.maximum(x, 0.0) + LN2 * jnp.log2(1.0 + jnp.exp2(-LOG2E * jnp.abs(x)))


def _log_sigmoid(x):
    return jnp.minimum(x, 0.0) - LN2 * jnp.log2(1.0 + jnp.exp2(-LOG2E * jnp.abs(x)))


def _split3(x):
    hi = x.astype(BF16)
    r1 = x - hi.astype(F32)
    mid = r1.astype(BF16)
    lo = (r1 - mid.astype(F32)).astype(BF16)
    return hi, mid, lo


def _dot01_left(a01, x):
    return sum(jnp.dot(a01, p, preferred_element_type=F32) for p in _split3(x))


def _dot01_right(x, a01):
    return sum(jnp.dot(p, a01, preferred_element_type=F32) for p in _split3(x))


def _dot_nt(a, b):
    return lax.dot_general(a, b, (((1,), (1,)), ((), ())), preferred_element_type=F32)


def _tril01(t):
    r = lax.broadcasted_iota(jnp.int32, (t, t), 0)
    c = lax.broadcasted_iota(jnp.int32, (t, t), 1)
    return r >= c


def _causal_conv(x, ext_ref, s, cw, cb):
    t = x.shape[0]
    ext_ref[s, SUBLANES:SUBLANES + t, :] = x
    acc = cb + cw[CONV_WIDTH - 1:CONV_WIDTH] * x
    for k in range(1, CONV_WIDTH):
        acc = acc + cw[CONV_WIDTH - 1 - k:CONV_WIDTH - k] * ext_ref[s, SUBLANES - k:SUBLANES - k + t, :]
    ext_ref[s, 0:SUBLANES, :] = x[t - SUBLANES:t]
    return acc


def _cparams(sem):
    return pltpu.CompilerParams(dimension_semantics=sem, vmem_limit_bytes=VMEM_LIMIT)


def _rms_to_scratch(x_ref, nw_ref, h_ref):
    x = x_ref[...]
    ms = jnp.mean(x * x, axis=-1, keepdims=True)
    h_ref[...] = (x * lax.rsqrt(ms + NORM_EPS) * nw_ref[...]).astype(BF16)


def _inproj_kernel(x_ref, nw_ref, w_ref, ws_ref, o_ref, os_ref, h_ref):
    @pl.when(pl.program_id(1) == 0)
    def _():
        _rms_to_scratch(x_ref, nw_ref, h_ref)
        os_ref[...] = jnp.dot(h_ref[...], ws_ref[...], preferred_element_type=F32)

    o_ref[...] = jnp.dot(h_ref[...], w_ref[...], preferred_element_type=F32)


def _inproj(x, nw, w, ws, *, tm, tn):
    m, k = x.shape
    n = w.shape[1]
    tm = min(tm, m)
    return pl.pallas_call(
        _inproj_kernel,
        grid=(m // tm, n // tn),
        in_specs=[pl.BlockSpec((tm, k), lambda i, j: (i, 0)),
                  pl.BlockSpec((1, k), lambda i, j: (0, 0)),
                  pl.BlockSpec((k, tn), lambda i, j: (0, j)),
                  pl.BlockSpec((k, LANES), lambda i, j: (0, 0))],
        out_specs=[pl.BlockSpec((tm, tn), lambda i, j: (i, j)),
                   pl.BlockSpec((tm, LANES), lambda i, j: (i, 0))],
        out_shape=[jax.ShapeDtypeStruct((m, n), F32), jax.ShapeDtypeStruct((m, LANES), F32)],
        scratch_shapes=[pltpu.VMEM((tm, k), BF16)],
        compiler_params=_cparams(("arbitrary", "arbitrary")),
        name="inproj",
    )(x, nw, w, ws)


def _up_kernel(x_ref, nw_ref, w_ref, o_ref, h_ref):
    @pl.when(pl.program_id(1) == 0)
    def _():
        _rms_to_scratch(x_ref, nw_ref, h_ref)

    acc = jnp.dot(h_ref[...], w_ref[...], preferred_element_type=F32)
    o_ref[...] = jnp.square(jnp.maximum(acc, 0.0)).astype(o_ref.dtype)


def _up(x, nw, w, *, tm, tn):
    m, k = x.shape
    n = w.shape[1]
    tm = min(tm, m)
    return pl.pallas_call(
        _up_kernel,
        grid=(m // tm, n // tn),
        in_specs=[pl.BlockSpec((tm, k), lambda i, j: (i, 0)),
                  pl.BlockSpec((1, k), lambda i, j: (0, 0)),
                  pl.BlockSpec((k, tn), lambda i, j: (0, j))],
        out_specs=pl.BlockSpec((tm, tn), lambda i, j: (i, j)),
        out_shape=jax.ShapeDtypeStruct((m, n), BF16),
        scratch_shapes=[pltpu.VMEM((tm, k), BF16)],
        compiler_params=_cparams(("arbitrary", "arbitrary")),
        name="mlp_up",
    )(x, nw, w)


def _ple_kernel(x_ref, nw_ref, wg_ref, p_ref, wp_ref, o_ref, h_ref, *, tn):
    j = pl.program_id(1)

    @pl.when(j == 0)
    def _():
        _rms_to_scratch(x_ref, nw_ref, h_ref)

    gate = jnp.dot(h_ref[...], wg_ref[...], preferred_element_type=F32)
    emb = jnp.dot(p_ref[...].astype(BF16), wp_ref[...], preferred_element_type=F32)
    xc = x_ref[:, pl.ds(pl.multiple_of(j * tn, tn), tn)]
    o_ref[...] = xc + emb * _sigmoid(gate)


def _ple(x, nw, wg, p, wp, *, tm, tn):
    m, k = x.shape
    tm = min(tm, m)
    kp = p.shape[1]
    return pl.pallas_call(
        functools.partial(_ple_kernel, tn=tn),
        grid=(m // tm, k // tn),
        in_specs=[pl.BlockSpec((tm, k), lambda i, j: (i, 0)),
                  pl.BlockSpec((1, k), lambda i, j: (0, 0)),
                  pl.BlockSpec((k, tn), lambda i, j: (0, j)),
                  pl.BlockSpec((tm, kp), lambda i, j: (i, 0)),
                  pl.BlockSpec((kp, tn), lambda i, j: (0, j))],
        out_specs=pl.BlockSpec((tm, tn), lambda i, j: (i, j)),
        out_shape=jax.ShapeDtypeStruct((m, k), F32),
        scratch_shapes=[pltpu.VMEM((tm, k), BF16)],
        compiler_params=_cparams(("arbitrary", "arbitrary")),
        name="ple",
    )(x, nw, wg, p, wp)


def _ple_final_kernel(x_ref, nw_ref, wg_ref, p_ref, wp_ref, fw_ref, o_ref, h_ref, ss_ref, *, tn):
    j = pl.program_id(1)

    @pl.when(j == 0)
    def _():
        _rms_to_scratch(x_ref, nw_ref, h_ref)
        ss_ref[...] = jnp.zeros_like(ss_ref)

    gate = jnp.dot(h_ref[...], wg_ref[...], preferred_element_type=F32)
    emb = jnp.dot(p_ref[...].astype(BF16), wp_ref[...], preferred_element_type=F32)
    cols = pl.ds(pl.multiple_of(j * tn, tn), tn)
    y = x_ref[:, cols] + emb * _sigmoid(gate)
    o_ref[:, cols] = y
    ss_ref[...] += jnp.sum(y * y, axis=-1, keepdims=True)

    @pl.when(j == pl.num_programs(1) - 1)
    def _():
        scale = lax.rsqrt(ss_ref[...] * (1.0 / o_ref.shape[1]) + NORM_EPS)
        o_ref[...] = o_ref[...] * scale * fw_ref[...]


def _ple_final(x, nw, wg, p, wp, fw, *, tm, tn):
    m, k = x.shape
    tm = min(tm, m)
    kp = p.shape[1]
    return pl.pallas_call(
        functools.partial(_ple_final_kernel, tn=tn),
        grid=(m // tm, k // tn),
        in_specs=[pl.BlockSpec((tm, k), lambda i, j: (i, 0)),
                  pl.BlockSpec((1, k), lambda i, j: (0, 0)),
                  pl.BlockSpec((k, tn), lambda i, j: (0, j)),
                  pl.BlockSpec((tm, kp), lambda i, j: (i, 0)),
                  pl.BlockSpec((kp, tn), lambda i, j: (0, j)),
                  pl.BlockSpec((1, k), lambda i, j: (0, 0))],
        out_specs=pl.BlockSpec((tm, k), lambda i, j: (i, 0)),
        out_shape=jax.ShapeDtypeStruct((m, k), F32),
        scratch_shapes=[pltpu.VMEM((tm, k), BF16), pltpu.VMEM((tm, 1), F32)],
        compiler_params=_cparams(("arbitrary", "arbitrary")),
        name="ple_final",
    )(x, nw, wg, p, wp, fw)


def _mm_res_kernel(a_ref, w_ref, r_ref, o_ref, acc_ref):
    kk = pl.program_id(2)

    @pl.when(kk == 0)
    def _():
        acc_ref[...] = r_ref[...]

    acc_ref[...] += jnp.dot(a_ref[...], w_ref[...], preferred_element_type=F32)

    @pl.when(kk == pl.num_programs(2) - 1)
    def _():
        o_ref[...] = acc_ref[...]


def _mm_res(a, w, r, *, tm, tn, tk):
    m, k = a.shape
    n = w.shape[1]
    tm = min(tm, m)
    return pl.pallas_call(
        _mm_res_kernel,
        grid=(m // tm, n // tn, k // tk),
        in_specs=[pl.BlockSpec((tm, tk), lambda i, j, q: (i, q)),
                  pl.BlockSpec((tk, tn), lambda i, j, q: (q, j)),
                  pl.BlockSpec((tm, tn), lambda i, j, q: (i, j))],
        out_specs=pl.BlockSpec((tm, tn), lambda i, j, q: (i, j)),
        out_shape=jax.ShapeDtypeStruct((m, n), F32),
        scratch_shapes=[pltpu.VMEM((tm, tn), F32)],
        compiler_params=_cparams(("arbitrary", "arbitrary", "arbitrary")),
        name="mm_res",
    )(a, w, r)


def _merge_kernel(ya_ref, yb_ref, yc_ref, yd_ref, ga_ref, gb_ref, gc_ref, gd_ref, wb_ref, o_ref):
    acc = None
    for br, (y_ref, g_ref) in enumerate(((ya_ref, ga_ref), (yb_ref, gb_ref), (yc_ref, gc_ref), (yd_ref, gd_ref))):
        t = _sigmoid(g_ref[...]) * jnp.dot(y_ref[...], wb_ref[br], preferred_element_type=F32)
        acc = t if acc is None else acc + t
    o_ref[...] = acc.astype(o_ref.dtype)


def _merge(ys, proj2d, wb, *, tm, tn):
    m = proj2d.shape[0]
    tm = min(tm, m)
    y_spec = pl.BlockSpec((tm, BRANCH_WIDTH), lambda i, j: (i, 0))
    gate_specs = [pl.BlockSpec((tm, tn), functools.partial(
        lambda i, j, off: (i, off + j), off=(COL_GATE + br * D_MODEL) // tn)) for br in range(N_BRANCH)]
    return pl.pallas_call(
        _merge_kernel,
        grid=(m // tm, D_MODEL // tn),
        in_specs=[y_spec] * N_BRANCH + gate_specs + [pl.BlockSpec((N_BRANCH, BRANCH_WIDTH, tn), lambda i, j: (0, 0, j))],
        out_specs=pl.BlockSpec((tm, tn), lambda i, j: (i, j)),
        out_shape=jax.ShapeDtypeStruct((m, D_MODEL), BF16),
        compiler_params=_cparams(("arbitrary", "arbitrary")),
        name="merge",
    )(*ys, proj2d, proj2d, proj2d, proj2d, wb)


def _seg_spec(nb, t, width, col):
    return pl.BlockSpec((nb, t, width), functools.partial(lambda i, c, cb: (i, c, cb), cb=col // width))


def _const_spec(shape):
    return pl.BlockSpec(shape, lambda i, c: (0,) * len(shape))


def _ssd_kernel(xbc_ref, z_ref, sm_ref, cw_ref, cb_ref, dtb_ref, alog_ref, dsk_ref, nw_ref, e_ref,
                o_ref, ext_ref, state_ref):
    t = SSM_CHUNK
    gw = BRANCH_WIDTH // SSM_GROUPS
    heads_per_group = SSM_HEADS // SSM_GROUPS

    @pl.when(pl.program_id(1) == 0)
    def _():
        ext_ref[:, 0:SUBLANES, :] = jnp.zeros((ext_ref.shape[0], SUBLANES, ext_ref.shape[2]), F32)
        state_ref[...] = jnp.zeros_like(state_ref)

    causal = _tril01(t)
    causal16 = causal.astype(BF16)
    e01 = e_ref[...]
    lane_hi = lax.broadcasted_iota(jnp.int32, (1, LANES), 1) >= SSM_HEAD_DIM
    neg_a = -jnp.exp(alog_ref[...])
    for s in range(xbc_ref.shape[0]):
        u = _silu(_causal_conv(xbc_ref[s], ext_ref, s, cw_ref[...], cb_ref[...]))
        xs = u[:, :BRANCH_WIDTH]
        bm = u[:, BRANCH_WIDTH:BRANCH_WIDTH + SSM_GROUPS * SSM_STATE]
        cm = u[:, BRANCH_WIDTH + SSM_GROUPS * SSM_STATE:]

        dt = _softplus(sm_ref[s] + dtb_ref[...])
        cs = _dot01_left(causal16, dt * neg_a)
        cs_t = cs.T
        dt_t = dt.T
        tot = cs[t - 1:t]
        w_exp = _dot01_right(jnp.exp(tot - cs) * dt, e01)
        ecs_exp = _dot01_right(jnp.exp(cs), e01)
        etot_exp = _dot01_right(jnp.broadcast_to(jnp.exp(tot), (SUBLANES, LANES)), e01)[0:1]

        groups = range(SSM_GROUPS)
        gsls = [slice(g * gw, (g + 1) * gw) for g in groups]
        heads_per_pair = LANES // SSM_HEAD_DIM

        cb_mat, y_off = {}, {}
        for g in groups:
            bg = bm[:, g * SSM_STATE:(g + 1) * SSM_STATE]
            cg16 = cm[:, g * SSM_STATE:(g + 1) * SSM_STATE].astype(BF16)
            cb_mat[g] = _dot_nt(cg16, bg.astype(BF16))
            st = state_ref[s, g]
            y_off[g] = jnp.dot(cg16, st.astype(BF16), preferred_element_type=F32) * ecs_exp[:, gsls[g]]
            xw = (xs[:, gsls[g]] * w_exp[:, gsls[g]]).astype(BF16)
            state_ref[s, g] = st * etot_exp[:, gsls[g]] + jnp.dot(bg.T.astype(BF16), xw, preferred_element_type=F32)

        mix = {}
        for h in range(SSM_HEADS):
            seg = jnp.minimum(cs[:, h:h + 1] - cs_t[h:h + 1, :], 0.0)
            lmat = jnp.where(causal, jnp.exp(seg), 0.0)
            mix[h] = (cb_mat[h // heads_per_group] * lmat * dt_t[h:h + 1, :]).astype(BF16)

        y_diag = {}
        for pr in range(SSM_HEADS // heads_per_pair):
            xpair = xs[:, pr * LANES:(pr + 1) * LANES]
            acc = None
            for hh in range(heads_per_pair):
                xm = jnp.where(lane_hi if hh else jnp.logical_not(lane_hi), xpair, 0.0).astype(BF16)
                part = jnp.dot(mix[pr * heads_per_pair + hh], xm, preferred_element_type=F32)
                acc = part if acc is None else acc + part
            y_diag[pr] = acc

        z = z_ref[s]
        pairs_per_group = gw // LANES
        for g in groups:
            gsl = gsls[g]
            y = y_off[g] + jnp.concatenate([y_diag[g * pairs_per_group + pr] for pr in range(pairs_per_group)], axis=1)
            y = (y + dsk_ref[:, gsl] * xs[:, gsl]) * _silu(z[:, gsl])
            ms = jnp.mean(y * y, axis=-1, keepdims=True)
            o_ref[s, :, gsl] = (y * lax.rsqrt(ms + NORM_EPS) * nw_ref[:, gsl]).astype(o_ref.dtype)


def _ssd(proj, small, cw, cb, dtb, alog, dsk, nw, e01):
    b, l, _ = proj.shape
    t = SSM_CHUNK
    nb = min(SEQS_PER_STEP, b)
    return pl.pallas_call(
        _ssd_kernel,
        grid=(b // nb, l // t),
        in_specs=[_seg_spec(nb, t, SSM_CONV_DIM, COL_XBC), _seg_spec(nb, t, BRANCH_WIDTH, COL_Z),
                  pl.BlockSpec((nb, t, LANES), lambda i, c: (i, c, 0)),
                  _const_spec((CONV_WIDTH, SSM_CONV_DIM)), _const_spec((1, SSM_CONV_DIM)),
                  _const_spec((1, LANES)), _const_spec((1, LANES)),
                  _const_spec((1, BRANCH_WIDTH)), _const_spec((1, BRANCH_WIDTH)), _const_spec((LANES, BRANCH_WIDTH))],
        out_specs=pl.BlockSpec((nb, t, BRANCH_WIDTH), lambda i, c: (i, c, 0)),
        out_shape=jax.ShapeDtypeStruct((b, l, BRANCH_WIDTH), BF16),
        scratch_shapes=[pltpu.VMEM((nb, SUBLANES + t, SSM_CONV_DIM), F32),
                        pltpu.VMEM((nb, SSM_GROUPS, SSM_STATE, BRANCH_WIDTH // SSM_GROUPS), F32)],
        compiler_params=_cparams(("arbitrary", "arbitrary")),
        name="ssd",
    )(proj, proj, small, cw, cb, dtb, alog, dsk, nw, e01)


def _hgrn_kernel(q_ref, f_ref, i_ref, g_ref, lbl_ref, nw_ref, o_ref, st_ref, k_s, gc_s, gk_s, lk_s, *, layer):
    t = HGRN_CHUNK
    sub = HGRN_SUB
    half = SUBLANES
    hd = HGRN_HEAD_DIM

    @pl.when(pl.program_id(1) == 0)
    def _():
        st_ref[...] = jnp.zeros_like(st_ref)

    logits = lbl_ref[...]
    ex = jnp.exp(logits - jnp.max(logits, axis=0, keepdims=True))
    sm = ex / jnp.sum(ex, axis=0, keepdims=True)
    lb = jnp.zeros((1, BRANCH_WIDTH), F32)
    for r in range(1, layer + 1):
        lb = lb + sm[r:r + 1]
    la = jnp.log(lb)
    l1 = jnp.log1p(-lb)

    tril16 = _tril01(t).astype(BF16)
    ones16 = jnp.ones((hd, hd), BF16)
    lane_h = lax.broadcasted_iota(jnp.int32, (half, t), 1)
    row_h = lax.broadcasted_iota(jnp.int32, (half, t), 0)
    lane_s = lax.broadcasted_iota(jnp.int32, (sub, t), 1)
    for s in range(q_ref.shape[0]):
        fp = f_ref[s]
        ls = _log_sigmoid(fp)
        lbb = l1 + ls
        logf = jnp.maximum(la, lbb) + LN2 * jnp.log2(1.0 + jnp.exp2(-LOG2E * jnp.abs(la - lbb)))
        k_s[s] = (1.0 - lb) * _sigmoid(-fp)
        lk = (l1 + (ls - fp)) * LOG2E
        gc2 = _dot01_left(tril16, logf) * LOG2E
        gc_s[s] = gc2
        gk_s[s] = gc2 - lk
        lk_s[s] = lk
        qq = _silu(q_ref[s]) * (hd ** -0.5)
        vv = i_ref[s]
        gg = g_ref[s]
        heads = range(HGRN_HEADS)
        subs = range(t // sub)
        sls = [slice(h * hd, (h + 1) * hd) for h in heads]

        o_inter, k_end, g_ends = {}, {}, {}
        for h in heads:
            sl = sls[h]
            k = k_s[s, :, sl]
            gc = gc_s[s, :, sl]
            glast = gc_s[s, t - 1:t, sl]
            st = st_ref[s, h]
            o_inter[h] = _dot_nt((qq[:, sl] * jnp.exp2(gc)).astype(BF16), st.astype(BF16))
            g_ends[h] = [gc_s[s, (sc + 1) * sub - 1:(sc + 1) * sub, sl] for sc in subs]
            g_end_rows = jnp.concatenate([jnp.broadcast_to(g, (sub, hd)) for g in g_ends[h]], axis=0)
            k_end[h] = k * jnp.exp2(g_end_rows - gc)
            kd = (k * jnp.exp2(glast - gc)).astype(BF16)
            st_ref[s, h] = st * jnp.exp2(glast) + jnp.dot(vv[:, sl].T.astype(BF16), kd, preferred_element_type=F32)

        rsum = {}
        for h in heads:
            sl = sls[h]
            for sc in subs:
                r0 = sc * sub
                q_lo, q_hi = qq[r0:r0 + half, sl], qq[r0 + half:r0 + sub, sl]
                g_lo, g_hi = gc_s[s, r0:r0 + half, sl], gc_s[s, r0 + half:r0 + sub, sl]
                prods = []
                for j in range(sub):
                    gk_j = gk_s[s, r0 + j:r0 + j + 1, sl]
                    lk_j = lk_s[s, r0 + j:r0 + j + 1, sl]
                    if j < half:
                        prods.append(q_lo * jnp.exp2(jnp.minimum(g_lo - gk_j, lk_j)))
                        prods.append(q_hi * jnp.exp2(g_hi - gk_j))
                    else:
                        prods.append(q_hi * jnp.exp2(jnp.minimum(g_hi - gk_j, lk_j)))
                rsum[h, sc] = jnp.dot(jnp.concatenate(prods, axis=0).astype(BF16), ones16, preferred_element_type=F32)

        off = {}
        for h in heads:
            sl = sls[h]
            for sc in subs[1:]:
                r0 = sc * sub
                g_r = gc_s[s, r0:r0 + 1, sl]
                qs = (qq[r0:r0 + sub, sl] * jnp.exp2(gc_s[s, r0:r0 + sub, sl] - g_r)).astype(BF16)
                ks = jnp.concatenate(
                    [k_end[h][pj * sub:(pj + 1) * sub] * jnp.exp2(g_r - g_ends[h][pj]) for pj in range(sc)]
                    + [k_end[h][r0:]], axis=0).astype(BF16)
                off[h, sc] = _dot_nt(qs, ks)

        for h in heads:
            sl = sls[h]
            a_rows = []
            for sc in subs:
                r0 = sc * sub
                blk_lo = jnp.zeros((half, t), F32)
                blk_hi = jnp.zeros((half, t), F32)
                idx = 0
                for j in range(sub):
                    hit = lane_h == r0 + j
                    if j < half:
                        blk_lo = jnp.where(hit & (row_h >= j), rsum[h, sc][idx * half:(idx + 1) * half, :t], blk_lo)
                        blk_hi = jnp.where(hit, rsum[h, sc][(idx + 1) * half:(idx + 2) * half, :t], blk_hi)
                        idx += 2
                    else:
                        blk_hi = jnp.where(hit & (row_h >= j - half), rsum[h, sc][idx * half:(idx + 1) * half, :t], blk_hi)
                        idx += 1
                blk = jnp.concatenate([blk_lo, blk_hi], axis=0)
                if sc > 0:
                    blk = jnp.where(lane_s < r0, off[h, sc], blk)
                a_rows.append(blk)
            attn = jnp.concatenate(a_rows, axis=0)
            o = o_inter[h] + jnp.dot(attn.astype(BF16), vv[:, sl].astype(BF16), preferred_element_type=F32)
            ms = jnp.mean(o * o, axis=-1, keepdims=True)
            o = o * lax.rsqrt(ms + NORM_EPS) * nw_ref[:, sl]
            o_ref[s, :, sl] = (o * _silu(gg[:, sl])).astype(o_ref.dtype)


def _hgrn(proj, lb_logits, nw, layer):
    b, l, _ = proj.shape
    t = HGRN_CHUNK
    w = BRANCH_WIDTH
    nb = min(SEQS_PER_STEP, b)
    return pl.pallas_call(
        functools.partial(_hgrn_kernel, layer=layer),
        grid=(b // nb, l // t),
        in_specs=[_seg_spec(nb, t, w, COL_BQ), _seg_spec(nb, t, w, COL_BF), _seg_spec(nb, t, w, COL_BI),
                  _seg_spec(nb, t, w, COL_BG), _const_spec(lb_logits.shape), _const_spec((1, w))],
        out_specs=pl.BlockSpec((nb, t, w), lambda i, c: (i, c, 0)),
        out_shape=jax.ShapeDtypeStruct((b, l, w), BF16),
        scratch_shapes=[pltpu.VMEM((nb, HGRN_HEADS, HGRN_HEAD_DIM, HGRN_HEAD_DIM), F32),
                        pltpu.VMEM((nb, t, w), F32), pltpu.VMEM((nb, t, w), F32),
                        pltpu.VMEM((nb, t, w), F32), pltpu.VMEM((nb, t, w), F32)],
        compiler_params=_cparams(("arbitrary", "arbitrary")),
        name="hgrn2",
    )(proj, proj, proj, proj, lb_logits, nw)


def _mlstm_kernel(qk_ref, v_ref, og_ref, sm_ref, gb_ref, nw_ref, out_ref, c_ref, m_ref):
    t = MLSTM_CHUNK
    dk = MLSTM_QK_DIM
    dv = MLSTM_V_DIM

    @pl.when(pl.program_id(1) == 0)
    def _():
        c_ref[...] = jnp.zeros_like(c_ref)
        m_ref[...] = jnp.zeros_like(m_ref)

    causal = _tril01(t)
    causal16 = causal.astype(BF16)
    eye = (lax.broadcasted_iota(jnp.int32, (LANES, LANES), 0)
           == lax.broadcasted_iota(jnp.int32, (LANES, LANES), 1)).astype(BF16)
    lane_e = lax.broadcasted_iota(jnp.int32, (t, LANES), 1)
    ones_col = jnp.where(lane_e == 0, 1.0, 0.0).astype(BF16)
    items = [(s, h) for s in range(qk_ref.shape[0]) for h in range(MLSTM_HEADS)]
    gate_cols = {}
    for s in range(qk_ref.shape[0]):
        gates = sm_ref[s] + gb_ref[...]
        bcum = _dot01_left(causal16, _log_sigmoid(gates))
        bcum_t = sum(_dot_nt(eye, p) for p in _split3(bcum))
        gates_t = sum(_dot_nt(eye, p) for p in _split3(gates))
        for h in range(MLSTM_HEADS):
            gate_cols[s, h] = (bcum[:, SM_CF + h:SM_CF + h + 1],
                               gates[:, SM_CI + h:SM_CI + h + 1],
                               bcum_t[SM_CF + h:SM_CF + h + 1, :],
                               gates_t[SM_CI + h:SM_CI + h + 1, :],
                               bcum[t - 1:t, SM_CF + h:SM_CF + h + 1])

    q16, k32, v_ext, c_st, m_st, raw, inter = {}, {}, {}, {}, {}, {}, {}
    for it in items:
        s, h = it
        q16[it] = (qk_ref[s, :, h * dk:(h + 1) * dk] * (dk ** -0.5)).astype(BF16)
        k32[it] = qk_ref[s, :, (MLSTM_HEADS + h) * dk:(MLSTM_HEADS + h + 1) * dk]
        v_ext[it] = jnp.concatenate([v_ref[s, :, h * dv:(h + 1) * dv].astype(BF16), ones_col], axis=1)
        c_st[it] = c_ref[s, h]
        m_st[it] = m_ref[s, h][:, 0:1]
        raw[it] = _dot_nt(q16[it], k32[it].astype(BF16))
        inter[it] = jnp.dot(q16[it], c_st[it].astype(BF16), preferred_element_type=F32)

    m_row, w_intra, w_inter = {}, {}, {}
    for it in items:
        bc, ig, bc_r, ig_r, _ = gate_cols[it]
        dlog = jnp.where(causal, bc - bc_r + ig_r, -jnp.inf)
        inter_log = bc + m_st[it]
        m_row[it] = jnp.maximum(inter_log, jnp.max(dlog, axis=-1, keepdims=True))
        w_intra[it] = jnp.exp(dlog - m_row[it])
        w_inter[it] = jnp.exp(inter_log - m_row[it])

    hh = {}
    for it in items:
        sc = (raw[it] * w_intra[it]).astype(BF16)
        nd = jnp.dot(sc, v_ext[it], preferred_element_type=F32) + w_inter[it] * inter[it]
        hh[it] = nd[:, :dv] / jnp.maximum(jnp.abs(nd[:, dv:dv + 1]), jnp.exp(-m_row[it]))

    for it in items:
        s, h = it
        bc, ig, _, _, b_last = gate_cols[it]
        log_w = b_last - bc + ig
        m_new = jnp.maximum(b_last + m_st[it], jnp.max(log_w, axis=0, keepdims=True))
        wk = jnp.exp(log_w - m_new)
        decay = jnp.exp(b_last + m_st[it] - m_new)
        kv = jnp.dot(k32[it].T.astype(BF16), (wk * v_ext[it].astype(F32)).astype(BF16), preferred_element_type=F32)
        c_ref[s, h] = decay * c_st[it] + kv
        m_ref[s, h] = jnp.broadcast_to(m_new, (1, LANES))

    for it in items:
        s, h = it
        ms = jnp.mean(hh[it] * hh[it], axis=-1, keepdims=True)
        hn = hh[it] * lax.rsqrt(ms + NORM_EPS) * nw_ref[:, h * dv:(h + 1) * dv]
        og = og_ref[s, :, h * dv:(h + 1) * dv]
        out_ref[s, :, h * dv:(h + 1) * dv] = (hn * _sigmoid(og)).astype(out_ref.dtype)


def _mlstm(proj, small, gate_bias, nw):
    b, l, _ = proj.shape
    t = MLSTM_CHUNK
    w = BRANCH_WIDTH
    nb = min(MLSTM_SEQS_PER_STEP, b)
    return pl.pallas_call(
        _mlstm_kernel,
        grid=(b // nb, l // t),
        in_specs=[_seg_spec(nb, t, w, COL_CQK), _seg_spec(nb, t, w, COL_CV), _seg_spec(nb, t, w, COL_CO),
                  pl.BlockSpec((nb, t, LANES), lambda i, c: (i, c, 0)),
                  _const_spec((1, LANES)), _const_spec((1, w))],
        out_specs=pl.BlockSpec((nb, t, w), lambda i, c: (i, c, 0)),
        out_shape=jax.ShapeDtypeStruct((b, l, w), BF16),
        scratch_shapes=[pltpu.VMEM((nb, MLSTM_HEADS, MLSTM_QK_DIM, MLSTM_V_DIM + LANES), F32),
                        pltpu.VMEM((nb, MLSTM_HEADS, 1, LANES), F32)],
        compiler_params=_cparams(("arbitrary", "arbitrary")),
        name="mlstm",
    )(proj, proj, proj, small, gate_bias, nw)


def _lru_kernel(x_ref, g_ref, cw_ref, cb_ref, wa_ref, ba_ref, wi_ref, bi_ref, ap_ref, o_ref, ext_ref, h_ref):
    t = LRU_CHUNK
    bd = LRU_BLOCK_DIM

    @pl.when(pl.program_id(1) == 0)
    def _():
        ext_ref[:, 0:SUBLANES, :] = jnp.zeros((ext_ref.shape[0], SUBLANES, ext_ref.shape[2]), F32)
        h_ref[...] = jnp.zeros_like(h_ref)

    row_in = lax.broadcasted_iota(jnp.int32, (SUBLANES, 1), 0)
    neg_sp = -LRU_C * _softplus(-ap_ref[...])
    for s in range(x_ref.shape[0]):
        xc = _causal_conv(x_ref[s], ext_ref, s, cw_ref[...], cb_ref[...])
        xc16 = xc.astype(BF16)
        r_parts, i_parts = [], []
        for n in range(LRU_BLOCKS):
            xb = xc16[:, n * bd:(n + 1) * bd]
            r_parts.append(jnp.dot(xb, wa_ref[n], preferred_element_type=F32))
            i_parts.append(jnp.dot(xb, wi_ref[n], preferred_element_type=F32))
        r = _sigmoid(jnp.concatenate(r_parts, axis=1) + ba_ref[...])
        ig = _sigmoid(jnp.concatenate(i_parts, axis=1) + bi_ref[...])
        log_a = r * neg_sp
        a = jnp.exp(log_a)
        var = -jnp.tanh(log_a) * (a * a + 1.0)
        u = xc * ig * jnp.where(var > 0.0, var * lax.rsqrt(var), 0.0)

        carry = h_ref[s, 0:1]
        groups = []
        for r0 in range(0, t, SUBLANES):
            ug = u[r0:r0 + SUBLANES]
            ag = a[r0:r0 + SUBLANES]
            d = 1
            while d < SUBLANES:
                keep = row_in >= d
                ug = ug + ag * jnp.where(keep, pltpu.roll(ug, d, 0), 0.0)
                ag = ag * jnp.where(keep, pltpu.roll(ag, d, 0), 1.0)
                d *= 2
            hg = ug + ag * carry
            carry = hg[SUBLANES - 1:SUBLANES]
            groups.append(hg)
        h_ref[s] = jnp.broadcast_to(carry, h_ref.shape[1:])
        hseq = jnp.concatenate(groups, axis=0)
        o_ref[s] = (hseq * jax.nn.gelu(g_ref[s], approximate=True)).astype(o_ref.dtype)


def _lru(proj, cw, cb, wa, ba, wi, bi, ap):
    b, l, _ = proj.shape
    t = LRU_CHUNK
    w = BRANCH_WIDTH
    nb = min(SEQS_PER_STEP, b)
    row = _const_spec((1, w))
    blk = _const_spec((LRU_BLOCKS, LRU_BLOCK_DIM, LRU_BLOCK_DIM))
    return pl.pallas_call(
        _lru_kernel,
        grid=(b // nb, l // t),
        in_specs=[_seg_spec(nb, t, w, COL_DX), _seg_spec(nb, t, w, COL_DG), _const_spec((CONV_WIDTH, w)), row,
                  blk, row, blk, row, row],
        out_specs=pl.BlockSpec((nb, t, w), lambda i, c: (i, c, 0)),
        out_shape=jax.ShapeDtypeStruct((b, l, w), BF16),
        scratch_shapes=[pltpu.VMEM((nb, SUBLANES + t, w), F32), pltpu.VMEM((nb, SUBLANES, w), F32)],
        compiler_params=_cparams(("arbitrary", "arbitrary")),
        name="rglru",
    )(proj, proj, cw, cb, wa, ba, wi, bi, ap)


def _regroup_w_in(w):
    widths = (BRANCH_WIDTH, SSM_CONV_DIM, SSM_HEADS, BRANCH_WIDTH, BRANCH_WIDTH, BRANCH_WIDTH, BRANCH_WIDTH,
              MLSTM_HEADS * MLSTM_QK_DIM, MLSTM_HEADS * MLSTM_QK_DIM, BRANCH_WIDTH, BRANCH_WIDTH,
              MLSTM_HEADS, MLSTM_HEADS, BRANCH_WIDTH, BRANCH_WIDTH, N_BRANCH * D_MODEL)
    offs = [0]
    for wd in widths:
        offs.append(offs[-1] + wd)
    seg = [w[:, offs[i]:offs[i + 1]] for i in range(len(widths))]
    (a_z, a_xbc, a_dt, b_q, b_f, b_i, b_g, c_q, c_k, c_v, c_o, c_i, c_f, d_x, d_g, gate) = seg
    main = jnp.concatenate([a_xbc, a_z, b_q, b_f, b_i, b_g, c_q, c_k, c_v, c_o, d_x, d_g, gate], axis=1)
    pad = jnp.zeros((w.shape[0], LANES - SSM_HEADS - 2 * MLSTM_HEADS), w.dtype)
    small = jnp.concatenate([a_dt, c_i, c_f, pad], axis=1)
    return main.astype(BF16), small.astype(BF16)


def _pad_lanes(v, start):
    out = jnp.zeros((1, LANES), F32)
    return out.at[0, start:start + v.shape[0]].set(v.astype(F32))


def kernel(x, p, mix_norm, w_in, ssm_conv_w, ssm_conv_b, ssm_dt_bias, ssm_a_log, ssm_d, ssm_norm, hgrn_lb_logits, hgrn_norm, mlstm_i_bias, mlstm_f_bias, mlstm_norm, lru_conv_w, lru_conv_b, lru_wa, lru_ba, lru_wi, lru_bi, lru_a_param, w_branch, w_out, mlp_norm, w_up, w_down, ple_norm, w_ple, w_ple_gate, final_norm):
    b, l, d = x.shape
    depth = w_in.shape[0]
    n = b * l
    row = lambda v: v.astype(F32).reshape(1, -1)
    e01 = (jnp.arange(LANES)[:, None] == (jnp.arange(BRANCH_WIDTH)[None, :] // SSM_HEAD_DIM)).astype(BF16)
    w_branch16, w_out16, w_up16, w_down16 = (w.astype(BF16) for w in (w_branch, w_out, w_up, w_down))
    w_ple16, w_ple_gate16, lru_wa16, lru_wi16 = (w.astype(BF16) for w in (w_ple, w_ple_gate, lru_wa, lru_wi))

    xf = x.reshape(n, d)
    for i in range(depth):
        w_main, w_small = _regroup_w_in(w_in[i])
        proj, small = _inproj(xf, row(mix_norm[i]), w_main, w_small, tm=1024, tn=1024)
        proj3 = proj.reshape(b, l, P_COLS)
        small3 = small.reshape(b, l, LANES)

        y_a = _ssd(proj3, small3, ssm_conv_w[i].astype(F32), row(ssm_conv_b[i]), _pad_lanes(ssm_dt_bias[i], SM_DT),
                   _pad_lanes(ssm_a_log[i], SM_DT), row(jnp.repeat(ssm_d[i], SSM_HEAD_DIM)), row(ssm_norm[i]), e01)
        y_b = _hgrn(proj3, hgrn_lb_logits.astype(F32), row(hgrn_norm[i]), i)
        gate_bias = _pad_lanes(jnp.concatenate([mlstm_i_bias[i], mlstm_f_bias[i]]), SM_CI)
        y_c = _mlstm(proj3, small3, gate_bias, row(mlstm_norm[i]))
        y_d = _lru(proj3, lru_conv_w[i].astype(F32), row(lru_conv_b[i]), lru_wa16[i], row(lru_ba[i]),
                   lru_wi16[i], row(lru_bi[i]), row(lru_a_param[i]))

        ys = [y.reshape(n, BRANCH_WIDTH) for y in (y_a, y_b, y_c, y_d)]
        merged = _merge(ys, proj, w_branch16[i], tm=1024, tn=512)
        xf = _mm_res(merged, w_out16[i], xf, tm=1024, tn=1024, tk=D_MODEL)

        up = _up(xf, row(mlp_norm[i]), w_up16[i], tm=1024, tn=1024)
        xf = _mm_res(up, w_down16[i], xf, tm=1024, tn=1024, tk=2048)

        ple_args = (xf, row(ple_norm[i]), w_ple_gate16[i], p[i].reshape(n, PLE_DIM), w_ple16[i])
        if i + 1 < depth:
            xf = _ple(*ple_args, tm=1024, tn=1024)
        else:
            xf = _ple_final(*ple_args, row(final_norm), tm=512, tn=1024)
    return xf.reshape(b, l, d)
```

```python
import functools
import math

import jax
import jax.numpy as jnp
from jax import lax
from jax.experimental import pallas as pl
from jax.experimental.pallas import tpu as pltpu

F32 = jnp.float32
BF16 = jnp.bfloat16

D_MODEL = 2048
NORM_EPS = 1e-6
PLE_DIM = 256
N_BRANCH = 4
BRANCH_WIDTH = D_MODEL // 2
CONV_WIDTH = 4
SSM_HEAD_DIM = 64
SSM_HEADS = BRANCH_WIDTH // SSM_HEAD_DIM
SSM_GROUPS = 4
SSM_STATE = 128
SSM_CHUNK = 128
SSM_CONV_DIM = BRANCH_WIDTH + 2 * SSM_GROUPS * SSM_STATE
HGRN_HEAD_DIM = 128
HGRN_HEADS = BRANCH_WIDTH // HGRN_HEAD_DIM
HGRN_CHUNK = 64
HGRN_SUB = 16
MLSTM_HEADS = 4
MLSTM_QK_DIM = BRANCH_WIDTH // (2 * MLSTM_HEADS)
MLSTM_V_DIM = BRANCH_WIDTH // MLSTM_HEADS
MLSTM_CHUNK = 64
LRU_BLOCKS = 8
LRU_BLOCK_DIM = BRANCH_WIDTH // LRU_BLOCKS
LRU_C = 8.0
LRU_CHUNK = 128
D_FF = 4 * D_MODEL

LANES = 128
SUBLANES = 8
VMEM_LIMIT = 48 * 1024 * 1024
LOG2E = math.log2(math.e)
LN2 = math.log(2.0)
SEQS_PER_STEP = 4

COL_XBC = 0
COL_Z = COL_XBC + SSM_CONV_DIM
COL_BQ = COL_Z + BRANCH_WIDTH
COL_BF = COL_BQ + BRANCH_WIDTH
COL_BI = COL_BF + BRANCH_WIDTH
COL_BG = COL_BI + BRANCH_WIDTH
COL_CQK = COL_BG + BRANCH_WIDTH
COL_CV = COL_CQK + BRANCH_WIDTH
COL_CO = COL_CV + BRANCH_WIDTH
COL_DX = COL_CO + BRANCH_WIDTH
COL_DG = COL_DX + BRANCH_WIDTH
COL_GATE = COL_DG + BRANCH_WIDTH
P_COLS = COL_GATE + N_BRANCH * D_MODEL
SM_DT = 0
SM_CI = SSM_HEADS
SM_CF = SSM_HEADS + MLSTM_HEADS


def _sigmoid(x):
    return 1.0 / (1.0 + jnp.exp(-x))


def _silu(x):
    return x * _sigmoid(x)


def _softplus(x):
    return jnp.maximum(x, 0.0) + LN2 * jnp.log2(1.0 + jnp.exp2(-LOG2E * jnp.abs(x)))


def _log_sigmoid(x):
    return jnp.minimum(x, 0.0) - LN2 * jnp.log2(1.0 + jnp.exp2(-LOG2E * jnp.abs(x)))


def _split3(x):
    hi = x.astype(BF16)
    r1 = x - hi.astype(F32)
    mid = r1.astype(BF16)
    lo = (r1 - mid.astype(F32)).astype(BF16)
    return hi, mid, lo


def _dot01_left(a01, x):
    return sum(jnp.dot(a01, p, preferred_element_type=F32) for p in _split3(x))


def _dot01_right(x, a01):
    return sum(jnp.dot(p, a01, preferred_element_type=F32) for p in _split3(x))


def _dot_nt(a, b):
    return lax.dot_general(a, b, (((1,), (1,)), ((), ())), preferred_element_type=F32)


def _tril01(t):
    r = lax.broadcasted_iota(jnp.int32, (t, t), 0)
    c = lax.broadcasted_iota(jnp.int32, (t, t), 1)
    return r >= c


def _causal_conv(x, ext_ref, s, cw, cb):
    t = x.shape[0]
    ext_ref[s, SUBLANES:SUBLANES + t, :] = x
    acc = cb + cw[CONV_WIDTH - 1:CONV_WIDTH] * x
    for k in range(1, CONV_WIDTH):
        acc = acc + cw[CONV_WIDTH - 1 - k:CONV_WIDTH - k] * ext_ref[s, SUBLANES - k:SUBLANES - k + t, :]
    ext_ref[s, 0:SUBLANES, :] = x[t - SUBLANES:t]
    return acc


def _cparams(sem):
    return pltpu.CompilerParams(dimension_semantics=sem, vmem_limit_bytes=VMEM_LIMIT)


def _rms_to_scratch(x_ref, nw_ref, h_ref):
    x = x_ref[...]
    ms = jnp.mean(x * x, axis=-1, keepdims=True)
    h_ref[...] = (x * lax.rsqrt(ms + NORM_EPS) * nw_ref[...]).astype(BF16)


def _inproj_kernel(x_ref, nw_ref, w_ref, ws_ref, o_ref, os_ref, h_ref):
    @pl.when(pl.program_id(1) == 0)
    def _():
        _rms_to_scratch(x_ref, nw_ref, h_ref)
        os_ref[...] = jnp.dot(h_ref[...], ws_ref[...], preferred_element_type=F32)

    o_ref[...] = jnp.dot(h_ref[...], w_ref[...], preferred_element_type=F32)


def _inproj(x, nw, w, ws, *, tm, tn):
    m, k = x.shape
    n = w.shape[1]
    tm = min(tm, m)
    return pl.pallas_call(
        _inproj_kernel,
        grid=(m // tm, n // tn),
        in_specs=[pl.BlockSpec((tm, k), lambda i, j: (i, 0)),
                  pl.BlockSpec((1, k), lambda i, j: (0, 0)),
                  pl.BlockSpec((k, tn), lambda i, j: (0, j)),
                  pl.BlockSpec((k, LANES), lambda i, j: (0, 0))],
        out_specs=[pl.BlockSpec((tm, tn), lambda i, j: (i, j)),
                   pl.BlockSpec((tm, LANES), lambda i, j: (i, 0))],
        out_shape=[jax.ShapeDtypeStruct((m, n), F32), jax.ShapeDtypeStruct((m, LANES), F32)],
        scratch_shapes=[pltpu.VMEM((tm, k), BF16)],
        compiler_params=_cparams(("arbitrary", "arbitrary")),
        name="inproj",
    )(x, nw, w, ws)


def _up_kernel(x_ref, nw_ref, w_ref, o_ref, h_ref):
    @pl.when(pl.program_id(1) == 0)
    def _():
        _rms_to_scratch(x_ref, nw_ref, h_ref)

    acc = jnp.dot(h_ref[...], w_ref[...], preferred_element_type=F32)
    o_ref[...] = jnp.square(jnp.maximum(acc, 0.0)).astype(o_ref.dtype)


def _up(x, nw, w, *, tm, tn):
    m, k = x.shape
    n = w.shape[1]
    tm = min(tm, m)
    return pl.pallas_call(
        _up_kernel,
        grid=(m // tm, n // tn),
        in_specs=[pl.BlockSpec((tm, k), lambda i, j: (i, 0)),
                  pl.BlockSpec((1, k), lambda i, j: (0, 0)),
                  pl.BlockSpec((k, tn), lambda i, j: (0, j))],
        out_specs=pl.BlockSpec((tm, tn), lambda i, j: (i, j)),
        out_shape=jax.ShapeDtypeStruct((m, n), BF16),
        scratch_shapes=[pltpu.VMEM((tm, k), BF16)],
        compiler_params=_cparams(("arbitrary", "arbitrary")),
        name="mlp_up",
    )(x, nw, w)


def _ple_kernel(x_ref, nw_ref, wg_ref, p_ref, wp_ref, o_ref, h_ref, *, tn):
    j = pl.program_id(1)

    @pl.when(j == 0)
    def _():
        _rms_to_scratch(x_ref, nw_ref, h_ref)

    gate = jnp.dot(h_ref[...], wg_ref[...], preferred_element_type=F32)
    emb = jnp.dot(p_ref[...].astype(BF16), wp_ref[...], preferred_element_type=F32)
    xc = x_ref[:, pl.ds(pl.multiple_of(j * tn, tn), tn)]
    o_ref[...] = xc + emb * _sigmoid(gate)


def _ple(x, nw, wg, p, wp, *, tm, tn):
    m, k = x.shape
    tm = min(tm, m)
    kp = p.shape[1]
    return pl.pallas_call(
        functools.partial(_ple_kernel, tn=tn),
        grid=(m // tm, k // tn),
        in_specs=[pl.BlockSpec((tm, k), lambda i, j: (i, 0)),
                  pl.BlockSpec((1, k), lambda i, j: (0, 0)),
                  pl.BlockSpec((k, tn), lambda i, j: (0, j)),
                  pl.BlockSpec((tm, kp), lambda i, j: (i, 0)),
                  pl.BlockSpec((kp, tn), lambda i, j: (0, j))],
        out_specs=pl.BlockSpec((tm, tn), lambda i, j: (i, j)),
        out_shape=jax.ShapeDtypeStruct((m, k), F32),
        scratch_shapes=[pltpu.VMEM((tm, k), BF16)],
        compiler_params=_cparams(("arbitrary", "arbitrary")),
        name="ple",
    )(x, nw, wg, p, wp)


def _ple_final_kernel(x_ref, nw_ref, wg_ref, p_ref, wp_ref, fw_ref, o_ref, h_ref, ss_ref, *, tn):
    j = pl.program_id(1)

    @pl.when(j == 0)
    def _():
        _rms_to_scratch(x_ref, nw_ref, h_ref)
        ss_ref[...] = jnp.zeros_like(ss_ref)

    gate = jnp.dot(h_ref[...], wg_ref[...], preferred_element_type=F32)
    emb = jnp.dot(p_ref[...].astype(BF16), wp_ref[...], preferred_element_type=F32)
    cols = pl.ds(pl.multiple_of(j * tn, tn), tn)
    y = x_ref[:, cols] + emb * _sigmoid(gate)
    o_ref[:, cols] = y
    ss_ref[...] += jnp.sum(y * y, axis=-1, keepdims=True)

    @pl.when(j == pl.num_programs(1) - 1)
    def _():
        scale = lax.rsqrt(ss_ref[...] * (1.0 / o_ref.shape[1]) + NORM_EPS)
        o_ref[...] = o_ref[...] * scale * fw_ref[...]


def _ple_final(x, nw, wg, p, wp, fw, *, tm, tn):
    m, k = x.shape
    tm = min(tm, m)
    kp = p.shape[1]
    return pl.pallas_call(
        functools.partial(_ple_final_kernel, tn=tn),
        grid=(m // tm, k // tn),
        in_specs=[pl.BlockSpec((tm, k), lambda i, j: (i, 0)),
                  pl.BlockSpec((1, k), lambda i, j: (0, 0)),
                  pl.BlockSpec((k, tn), lambda i, j: (0, j)),
                  pl.BlockSpec((tm, kp), lambda i, j: (i, 0)),
                  pl.BlockSpec((kp, tn), lambda i, j: (0, j)),
                  pl.BlockSpec((1, k), lambda i, j: (0, 0))],
        out_specs=pl.BlockSpec((tm, k), lambda i, j: (i, 0)),
        out_shape=jax.ShapeDtypeStruct((m, k), F32),
        scratch_shapes=[pltpu.VMEM((tm, k), BF16), pltpu.VMEM((tm, 1), F32)],
        compiler_params=_cparams(("arbitrary", "arbitrary")),
        name="ple_final",
    )(x, nw, wg, p, wp, fw)


def _mm_res_kernel(a_ref, w_ref, r_ref, o_ref, acc_ref):
    kk = pl.program_id(2)

    @pl.when(kk == 0)
    def _():
        acc_ref[...] = r_ref[...]

    acc_ref[...] += jnp.dot(a_ref[...], w_ref[...], preferred_element_type=F32)

    @pl.when(kk == pl.num_programs(2) - 1)
    def _():
        o_ref[...] = acc_ref[...]


def _mm_res(a, w, r, *, tm, tn, tk):
    m, k = a.shape
    n = w.shape[1]
    tm = min(tm, m)
    return pl.pallas_call(
        _mm_res_kernel,
        grid=(m // tm, n // tn, k // tk),
        in_specs=[pl.BlockSpec((tm, tk), lambda i, j, q: (i, q)),
                  pl.BlockSpec((tk, tn), lambda i, j, q: (q, j)),
                  pl.BlockSpec((tm, tn), lambda i, j, q: (i, j))],
        out_specs=pl.BlockSpec((tm, tn), lambda i, j, q: (i, j)),
        out_shape=jax.ShapeDtypeStruct((m, n), F32),
        scratch_shapes=[pltpu.VMEM((tm, tn), F32)],
        compiler_params=_cparams(("arbitrary", "arbitrary", "arbitrary")),
        name="mm_res",
    )(a, w, r)


def _merge_kernel(ya_ref, yb_ref, yc_ref, yd_ref, ga_ref, gb_ref, gc_ref, gd_ref, wb_ref, o_ref):
    acc = None
    for br, (y_ref, g_ref) in enumerate(((ya_ref, ga_ref), (yb_ref, gb_ref), (yc_ref, gc_ref), (yd_ref, gd_ref))):
        t = _sigmoid(g_ref[...]) * jnp.dot(y_ref[...], wb_ref[br], preferred_element_type=F32)
        acc = t if acc is None else acc + t
    o_ref[...] = acc.astype(o_ref.dtype)


def _merge(ys, proj2d, wb, *, tm, tn):
    m = proj2d.shape[0]
    tm = min(tm, m)
    y_spec = pl.BlockSpec((tm, BRANCH_WIDTH), lambda i, j: (i, 0))
    gate_specs = [pl.BlockSpec((tm, tn), functools.partial(
        lambda i, j, off: (i, off + j), off=(COL_GATE + br * D_MODEL) // tn)) for br in range(N_BRANCH)]
    return pl.pallas_call(
        _merge_kernel,
        grid=(m // tm, D_MODEL // tn),
        in_specs=[y_spec] * N_BRANCH + gate_specs + [pl.BlockSpec((N_BRANCH, BRANCH_WIDTH, tn), lambda i, j: (0, 0, j))],
        out_specs=pl.BlockSpec((tm, tn), lambda i, j: (i, j)),
        out_shape=jax.ShapeDtypeStruct((m, D_MODEL), BF16),
        compiler_params=_cparams(("arbitrary", "arbitrary")),
        name="merge",
    )(*ys, proj2d, proj2d, proj2d, proj2d, wb)


def _seg_spec(nb, t, width, col):
    return pl.BlockSpec((nb, t, width), functools.partial(lambda i, c, cb: (i, c, cb), cb=col // width))


def _const_spec(shape):
    return pl.BlockSpec(shape, lambda i, c: (0,) * len(shape))


def _ssd_kernel(xbc_ref, z_ref, sm_ref, cw_ref, cb_ref, dtb_ref, alog_ref, dsk_ref, nw_ref, e_ref,
                o_ref, ext_ref, state_ref):
    t = SSM_CHUNK
    gw = BRANCH_WIDTH // SSM_GROUPS
    heads_per_group = SSM_HEADS // SSM_GROUPS

    @pl.when(pl.program_id(1) == 0)
    def _():
        ext_ref[:, 0:SUBLANES, :] = jnp.zeros((ext_ref.shape[0], SUBLANES, ext_ref.shape[2]), F32)
        state_ref[...] = jnp.zeros_like(state_ref)

    causal = _tril01(t)
    causal16 = causal.astype(BF16)
    e01 = e_ref[...]
    lane_hi = lax.broadcasted_iota(jnp.int32, (1, LANES), 1) >= SSM_HEAD_DIM
    neg_a = -jnp.exp(alog_ref[...])
    for s in range(xbc_ref.shape[0]):
        u = _silu(_causal_conv(xbc_ref[s], ext_ref, s, cw_ref[...], cb_ref[...]))
        xs = u[:, :BRANCH_WIDTH]
        bm = u[:, BRANCH_WIDTH:BRANCH_WIDTH + SSM_GROUPS * SSM_STATE]
        cm = u[:, BRANCH_WIDTH + SSM_GROUPS * SSM_STATE:]

        dt = _softplus(sm_ref[s] + dtb_ref[...])
        cs = _dot01_left(causal16, dt * neg_a)
        cs_t = cs.T
        dt_t = dt.T
        tot = cs[t - 1:t]
        w_exp = _dot01_right(jnp.exp(tot - cs) * dt, e01)
        ecs_exp = _dot01_right(jnp.exp(cs), e01)
        etot_exp = _dot01_right(jnp.broadcast_to(jnp.exp(tot), (SUBLANES, LANES)), e01)[0:1]

        groups = range(SSM_GROUPS)
        gsls = [slice(g * gw, (g + 1) * gw) for g in groups]
        heads_per_pair = LANES // SSM_HEAD_DIM

        cb_mat, y_off = {}, {}
        for g in groups:
            bg = bm[:, g * SSM_STATE:(g + 1) * SSM_STATE]
            cg16 = cm[:, g * SSM_STATE:(g + 1) * SSM_STATE].astype(BF16)
            cb_mat[g] = _dot_nt(cg16, bg.astype(BF16))
            st = state_ref[s, g]
            y_off[g] = jnp.dot(cg16, st.astype(BF16), preferred_element_type=F32) * ecs_exp[:, gsls[g]]
            xw = (xs[:, gsls[g]] * w_exp[:, gsls[g]]).astype(BF16)
            state_ref[s, g] = st * etot_exp[:, gsls[g]] + jnp.dot(bg.T.astype(BF16), xw, preferred_element_type=F32)

        mix = {}
        for h in range(SSM_HEADS):
            seg = jnp.minimum(cs[:, h:h + 1] - cs_t[h:h + 1, :], 0.0)
            lmat = jnp.where(causal, jnp.exp(seg), 0.0)
            mix[h] = (cb_mat[h // heads_per_group] * lmat * dt_t[h:h + 1, :]).astype(BF16)

        y_diag = {}
        for pr in range(SSM_HEADS // heads_per_pair):
            xpair = xs[:, pr * LANES:(pr + 1) * LANES]
            acc = None
            for hh in range(heads_per_pair):
                xm = jnp.where(lane_hi if hh else jnp.logical_not(lane_hi), xpair, 0.0).astype(BF16)
                part = jnp.dot(mix[pr * heads_per_pair + hh], xm, preferred_element_type=F32)
                acc = part if acc is None else acc + part
            y_diag[pr] = acc

        z = z_ref[s]
        pairs_per_group = gw // LANES
        for g in groups:
            gsl = gsls[g]
            y = y_off[g] + jnp.concatenate([y_diag[g * pairs_per_group + pr] for pr in range(pairs_per_group)], axis=1)
            y = (y + dsk_ref[:, gsl] * xs[:, gsl]) * _silu(z[:, gsl])
            ms = jnp.mean(y * y, axis=-1, keepdims=True)
            o_ref[s, :, gsl] = (y * lax.rsqrt(ms + NORM_EPS) * nw_ref[:, gsl]).astype(o_ref.dtype)


def _ssd(proj, small, cw, cb, dtb, alog, dsk, nw, e01):
    b, l, _ = proj.shape
    t = SSM_CHUNK
    nb = min(SEQS_PER_STEP, b)
    return pl.pallas_call(
        _ssd_kernel,
        grid=(b // nb, l // t),
        in_specs=[_seg_spec(nb, t, SSM_CONV_DIM, COL_XBC), _seg_spec(nb, t, BRANCH_WIDTH, COL_Z),
                  pl.BlockSpec((nb, t, LANES), lambda i, c: (i, c, 0)),
                  _const_spec((CONV_WIDTH, SSM_CONV_DIM)), _const_spec((1, SSM_CONV_DIM)),
                  _const_spec((1, LANES)), _const_spec((1, LANES)),
                  _const_spec((1, BRANCH_WIDTH)), _const_spec((1, BRANCH_WIDTH)), _const_spec((LANES, BRANCH_WIDTH))],
        out_specs=pl.BlockSpec((nb, t, BRANCH_WIDTH), lambda i, c: (i, c, 0)),
        out_shape=jax.ShapeDtypeStruct((b, l, BRANCH_WIDTH), BF16),
        scratch_shapes=[pltpu.VMEM((nb, SUBLANES + t, SSM_CONV_DIM), F32),
                        pltpu.VMEM((nb, SSM_GROUPS, SSM_STATE, BRANCH_WIDTH // SSM_GROUPS), F32)],
        compiler_params=_cparams(("arbitrary", "arbitrary")),
        name="ssd",
    )(proj, proj, small, cw, cb, dtb, alog, dsk, nw, e01)


def _hgrn_kernel(q_ref, f_ref, i_ref, g_ref, lbl_ref, nw_ref, o_ref, st_ref, k_s, gc_s, gk_s, lk_s, *, layer):
    t = HGRN_CHUNK
    sub = HGRN_SUB
    half = SUBLANES
    hd = HGRN_HEAD_DIM

    @pl.when(pl.program_id(1) == 0)
    def _():
        st_ref[...] = jnp.zeros_like(st_ref)

    logits = lbl_ref[...]
    ex = jnp.exp(logits - jnp.max(logits, axis=0, keepdims=True))
    sm = ex / jnp.sum(ex, axis=0, keepdims=True)
    lb = jnp.zeros((1, BRANCH_WIDTH), F32)
    for r in range(1, layer + 1):
        lb = lb + sm[r:r + 1]
    la = jnp.log(lb)
    l1 = jnp.log1p(-lb)

    tril16 = _tril01(t).astype(BF16)
    ones16 = jnp.ones((hd, hd), BF16)
    lane_h = lax.broadcasted_iota(jnp.int32, (half, t), 1)
    row_h = lax.broadcasted_iota(jnp.int32, (half, t), 0)
    lane_s = lax.broadcasted_iota(jnp.int32, (sub, t), 1)
    for s in range(q_ref.shape[0]):
        fp = f_ref[s]
        ls = _log_sigmoid(fp)
        lbb = l1 + ls
        logf = jnp.maximum(la, lbb) + LN2 * jnp.log2(1.0 + jnp.exp2(-LOG2E * jnp.abs(la - lbb)))
        k_s[s] = (1.0 - lb) * _sigmoid(-fp)
        lk = (l1 + (ls - fp)) * LOG2E
        gc2 = _dot01_left(tril16, logf) * LOG2E
        gc_s[s] = gc2
        gk_s[s] = gc2 - lk
        lk_s[s] = lk
        qq = _silu(q_ref[s]) * (hd ** -0.5)
        vv = i_ref[s]
        gg = g_ref[s]
        heads = range(HGRN_HEADS)
        subs = range(t // sub)
        sls = [slice(h * hd, (h + 1) * hd) for h in heads]

        o_inter, k_end, g_ends = {}, {}, {}
        for h in heads:
            sl = sls[h]
            k = k_s[s, :, sl]
            gc = gc_s[s, :, sl]
            glast = gc_s[s, t - 1:t, sl]
            st = st_ref[s, h]
            o_inter[h] = _dot_nt((qq[:, sl] * jnp.exp2(gc)).astype(BF16), st.astype(BF16))
            g_ends[h] = [gc_s[s, (sc + 1) * sub - 1:(sc + 1) * sub, sl] for sc in subs]
            g_end_rows = jnp.concatenate([jnp.broadcast_to(g, (sub, hd)) for g in g_ends[h]], axis=0)
            k_end[h] = k * jnp.exp2(g_end_rows - gc)
            kd = (k * jnp.exp2(glast - gc)).astype(BF16)
            st_ref[s, h] = st * jnp.exp2(glast) + jnp.dot(vv[:, sl].T.astype(BF16), kd, preferred_element_type=F32)

        rsum = {}
        for h in heads:
            sl = sls[h]
            for sc in subs:
                r0 = sc * sub
                q_lo, q_hi = qq[r0:r0 + half, sl], qq[r0 + half:r0 + sub, sl]
                g_lo, g_hi = gc_s[s, r0:r0 + half, sl], gc_s[s, r0 + half:r0 + sub, sl]
                prods = []
                for j in range(sub):
                    gk_j = gk_s[s, r0 + j:r0 + j + 1, sl]
                    lk_j = lk_s[s, r0 + j:r0 + j + 1, sl]
                    if j < half:
                        prods.append(q_lo * jnp.exp2(jnp.minimum(g_lo - gk_j, lk_j)))
                        prods.append(q_hi * jnp.exp2(g_hi - gk_j))
                    else:
                        prods.append(q_hi * jnp.exp2(jnp.minimum(g_hi - gk_j, lk_j)))
                rsum[h, sc] = jnp.dot(jnp.concatenate(prods, axis=0).astype(BF16), ones16, preferred_element_type=F32)

        off = {}
        for h in heads:
            sl = sls[h]
            for sc in subs[1:]:
                r0 = sc * sub
                g_r = gc_s[s, r0:r0 + 1, sl]
                qs = (qq[r0:r0 + sub, sl] * jnp.exp2(gc_s[s, r0:r0 + sub, sl] - g_r)).astype(BF16)
                ks = jnp.concatenate(
                    [k_end[h][pj * sub:(pj + 1) * sub] * jnp.exp2(g_r - g_ends[h][pj]) for pj in range(sc)]
                    + [k_end[h][r0:]], axis=0).astype(BF16)
                off[h, sc] = _dot_nt(qs, ks)

        for h in heads:
            sl = sls[h]
            a_rows = []
            for sc in subs:
                r0 = sc * sub
                blk_lo = jnp.zeros((half, t), F32)
                blk_hi = jnp.zeros((half, t), F32)
                idx = 0
                for j in range(sub):
                    hit = lane_h == r0 + j
                    if j < half:
                        blk_lo = jnp.where(hit & (row_h >= j), rsum[h, sc][idx * half:(idx + 1) * half, :t], blk_lo)
                        blk_hi = jnp.where(hit, rsum[h, sc][(idx + 1) * half:(idx + 2) * half, :t], blk_hi)
                        idx += 2
                    else:
                        blk_hi = jnp.where(hit & (row_h >= j - half), rsum[h, sc][idx * half:(idx + 1) * half, :t], blk_hi)
                        idx += 1
                blk = jnp.concatenate([blk_lo, blk_hi], axis=0)
                if sc > 0:
                    blk = jnp.where(lane_s < r0, off[h, sc], blk)
                a_rows.append(blk)
            attn = jnp.concatenate(a_rows, axis=0)
            o = o_inter[h] + jnp.dot(attn.astype(BF16), vv[:, sl].astype(BF16), preferred_element_type=F32)
            ms = jnp.mean(o * o, axis=-1, keepdims=True)
            o = o * lax.rsqrt(ms + NORM_EPS) * nw_ref[:, sl]
            o_ref[s, :, sl] = (o * _silu(gg[:, sl])).astype(o_ref.dtype)


def _hgrn(proj, lb_logits, nw, layer):
    b, l, _ = proj.shape
    t = HGRN_CHUNK
    w = BRANCH_WIDTH
    nb = min(SEQS_PER_STEP, b)
    return pl.pallas_call(
        functools.partial(_hgrn_kernel, layer=layer),
        grid=(b // nb, l // t),
        in_specs=[_seg_spec(nb, t, w, COL_BQ), _seg_spec(nb, t, w, COL_BF), _seg_spec(nb, t, w, COL_BI),
                  _seg_spec(nb, t, w, COL_BG), _const_spec(lb_logits.shape), _const_spec((1, w))],
        out_specs=pl.BlockSpec((nb, t, w), lambda i, c: (i, c, 0)),
        out_shape=jax.ShapeDtypeStruct((b, l, w), BF16),
        scratch_shapes=[pltpu.VMEM((nb, HGRN_HEADS, HGRN_HEAD_DIM, HGRN_HEAD_DIM), F32),
                        pltpu.VMEM((nb, t, w), F32), pltpu.VMEM((nb, t, w), F32),
                        pltpu.VMEM((nb, t, w), F32), pltpu.VMEM((nb, t, w), F32)],
        compiler_params=_cparams(("arbitrary", "arbitrary")),
        name="hgrn2",
    )(proj, proj, proj, proj, lb_logits, nw)


def _mlstm_kernel(qk_ref, v_ref, og_ref, sm_ref, gb_ref, nw_ref, out_ref, c_ref, m_ref):
    t = MLSTM_CHUNK
    dk = MLSTM_QK_DIM
    dv = MLSTM_V_DIM

    @pl.when(pl.program_id(1) == 0)
    def _():
        c_ref[...] = jnp.zeros_like(c_ref)
        m_ref[...] = jnp.zeros_like(m_ref)

    causal = _tril01(t)
    causal16 = causal.astype(BF16)
    eye = (lax.broadcasted_iota(jnp.int32, (LANES, LANES), 0)
           == lax.broadcasted_iota(jnp.int32, (LANES, LANES), 1)).astype(BF16)
    lane_e = lax.broadcasted_iota(jnp.int32, (t, LANES), 1)
    ones_col = jnp.where(lane_e == 0, 1.0, 0.0).astype(BF16)
    items = [(s, h) for s in range(qk_ref.shape[0]) for h in range(MLSTM_HEADS)]
    gate_cols = {}
    for s in range(qk_ref.shape[0]):
        gates = sm_ref[s] + gb_ref[...]
        bcum = _dot01_left(causal16, _log_sigmoid(gates))
        bcum_t = sum(_dot_nt(eye, p) for p in _split3(bcum))
        gates_t = sum(_dot_nt(eye, p) for p in _split3(gates))
        for h in range(MLSTM_HEADS):
            gate_cols[s, h] = (bcum[:, SM_CF + h:SM_CF + h + 1],
                               gates[:, SM_CI + h:SM_CI + h + 1],
                               bcum_t[SM_CF + h:SM_CF + h + 1, :],
                               gates_t[SM_CI + h:SM_CI + h + 1, :],
                               bcum[t - 1:t, SM_CF + h:SM_CF + h + 1])

    q16, k32, v_ext, c_st, m_st, raw, inter = {}, {}, {}, {}, {}, {}, {}
    for it in items:
        s, h = it
        q16[it] = (qk_ref[s, :, h * dk:(h + 1) * dk] * (dk ** -0.5)).astype(BF16)
        k32[it] = qk_ref[s, :, (MLSTM_HEADS + h) * dk:(MLSTM_HEADS + h + 1) * dk]
        v_ext[it] = jnp.concatenate([v_ref[s, :, h * dv:(h + 1) * dv].astype(BF16), ones_col], axis=1)
        c_st[it] = c_ref[s, h]
        m_st[it] = m_ref[s, h][:, 0:1]
        raw[it] = _dot_nt(q16[it], k32[it].astype(BF16))
        inter[it] = jnp.dot(q16[it], c_st[it].astype(BF16), preferred_element_type=F32)

    m_row, w_intra, w_inter = {}, {}, {}
    for it in items:
        bc, ig, bc_r, ig_r, _ = gate_cols[it]
        dlog = jnp.where(causal, bc - bc_r + ig_r, -jnp.inf)
        inter_log = bc + m_st[it]
        m_row[it] = jnp.maximum(inter_log, jnp.max(dlog, axis=-1, keepdims=True))
        w_intra[it] = jnp.exp(dlog - m_row[it])
        w_inter[it] = jnp.exp(inter_log - m_row[it])

    hh = {}
    for it in items:
        sc = (raw[it] * w_intra[it]).astype(BF16)
        nd = jnp.dot(sc, v_ext[it], preferred_element_type=F32) + w_inter[it] * inter[it]
        hh[it] = nd[:, :dv] / jnp.maximum(jnp.abs(nd[:, dv:dv + 1]), jnp.exp(-m_row[it]))

    for it in items:
        s, h = it
        bc, ig, _, _, b_last = gate_cols[it]
        log_w = b_last - bc + ig
        m_new = jnp.maximum(b_last + m_st[it], jnp.max(log_w, axis=0, keepdims=True))
        wk = jnp.exp(log_w - m_new)
        decay = jnp.exp(b_last + m_st[it] - m_new)
        kv = jnp.dot(k32[it].T.astype(BF16), (wk * v_ext[it].astype(F32)).astype(BF16), preferred_element_type=F32)
        c_ref[s, h] = decay * c_st[it] + kv
        m_ref[s, h] = jnp.broadcast_to(m_new, (1, LANES))

    for it in items:
        s, h = it
        ms = jnp.mean(hh[it] * hh[it], axis=-1, keepdims=True)
        hn = hh[it] * lax.rsqrt(ms + NORM_EPS) * nw_ref[:, h * dv:(h + 1) * dv]
        og = og_ref[s, :, h * dv:(h + 1) * dv]
        out_ref[s, :, h * dv:(h + 1) * dv] = (hn * _sigmoid(og)).astype(out_ref.dtype)


def _mlstm(proj, small, gate_bias, nw):
    b, l, _ = proj.shape
    t = MLSTM_CHUNK
    w = BRANCH_WIDTH
    nb = min(SEQS_PER_STEP, b)
    return pl.pallas_call(
        _mlstm_kernel,
        grid=(b // nb, l // t),
        in_specs=[_seg_spec(nb, t, w, COL_CQK), _seg_spec(nb, t, w, COL_CV), _seg_spec(nb, t, w, COL_CO),
                  pl.BlockSpec((nb, t, LANES), lambda i, c: (i, c, 0)),
                  _const_spec((1, LANES)), _const_spec((1, w))],
        out_specs=pl.BlockSpec((nb, t, w), lambda i, c: (i, c, 0)),
        out_shape=jax.ShapeDtypeStruct((b, l, w), BF16),
        scratch_shapes=[pltpu.VMEM((nb, MLSTM_HEADS, MLSTM_QK_DIM, MLSTM_V_DIM + LANES), F32),
                        pltpu.VMEM((nb, MLSTM_HEADS, 1, LANES), F32)],
        compiler_params=_cparams(("arbitrary", "arbitrary")),
        name="mlstm",
    )(proj, proj, proj, small, gate_bias, nw)


def _lru_kernel(x_ref, g_ref, cw_ref, cb_ref, wa_ref, ba_ref, wi_ref, bi_ref, ap_ref, o_ref, ext_ref, h_ref):
    t = LRU_CHUNK
    bd = LRU_BLOCK_DIM

    @pl.when(pl.program_id(1) == 0)
    def _():
        ext_ref[:, 0:SUBLANES, :] = jnp.zeros((ext_ref.shape[0], SUBLANES, ext_ref.shape[2]), F32)
        h_ref[...] = jnp.zeros_like(h_ref)

    row_in = lax.broadcasted_iota(jnp.int32, (SUBLANES, 1), 0)
    neg_sp = -LRU_C * _softplus(-ap_ref[...])
    for s in range(x_ref.shape[0]):
        xc = _causal_conv(x_ref[s], ext_ref, s, cw_ref[...], cb_ref[...])
        xc16 = xc.astype(BF16)
        r_parts, i_parts = [], []
        for n in range(LRU_BLOCKS):
            xb = xc16[:, n * bd:(n + 1) * bd]
            r_parts.append(jnp.dot(xb, wa_ref[n], preferred_element_type=F32))
            i_parts.append(jnp.dot(xb, wi_ref[n], preferred_element_type=F32))
        r = _sigmoid(jnp.concatenate(r_parts, axis=1) + ba_ref[...])
        ig = _sigmoid(jnp.concatenate(i_parts, axis=1) + bi_ref[...])
        log_a = r * neg_sp
        a = jnp.exp(log_a)
        var = -jnp.tanh(log_a) * (a * a + 1.0)
        u = xc * ig * jnp.where(var > 0.0, var * lax.rsqrt(var), 0.0)

        carry = h_ref[s, 0:1]
        groups = []
        for r0 in range(0, t, SUBLANES):
            ug = u[r0:r0 + SUBLANES]
            ag = a[r0:r0 + SUBLANES]
            d = 1
            while d < SUBLANES:
                keep = row_in >= d
                ug = ug + ag * jnp.where(keep, pltpu.roll(ug, d, 0), 0.0)
                ag = ag * jnp.where(keep, pltpu.roll(ag, d, 0), 1.0)
                d *= 2
            hg = ug + ag * carry
            carry = hg[SUBLANES - 1:SUBLANES]
            groups.append(hg)
        h_ref[s] = jnp.broadcast_to(carry, h_ref.shape[1:])
        hseq = jnp.concatenate(groups, axis=0)
        o_ref[s] = (hseq * jax.nn.gelu(g_ref[s], approximate=True)).astype(o_ref.dtype)


def _lru(proj, cw, cb, wa, ba, wi, bi, ap):
    b, l, _ = proj.shape
    t = LRU_CHUNK
    w = BRANCH_WIDTH
    nb = min(SEQS_PER_STEP, b)
    row = _const_spec((1, w))
    blk = _const_spec((LRU_BLOCKS, LRU_BLOCK_DIM, LRU_BLOCK_DIM))
    return pl.pallas_call(
        _lru_kernel,
        grid=(b // nb, l // t),
        in_specs=[_seg_spec(nb, t, w, COL_DX), _seg_spec(nb, t, w, COL_DG), _const_spec((CONV_WIDTH, w)), row,
                  blk, row, blk, row, row],
        out_specs=pl.BlockSpec((nb, t, w), lambda i, c: (i, c, 0)),
        out_shape=jax.ShapeDtypeStruct((b, l, w), BF16),
        scratch_shapes=[pltpu.VMEM((nb, SUBLANES + t, w), F32), pltpu.VMEM((nb, SUBLANES, w), F32)],
        compiler_params=_cparams(("arbitrary", "arbitrary")),
        name="rglru",
    )(proj, proj, cw, cb, wa, ba, wi, bi, ap)


def _regroup_w_in(w):
    widths = (BRANCH_WIDTH, SSM_CONV_DIM, SSM_HEADS, BRANCH_WIDTH, BRANCH_WIDTH, BRANCH_WIDTH, BRANCH_WIDTH,
              MLSTM_HEADS * MLSTM_QK_DIM, MLSTM_HEADS * MLSTM_QK_DIM, BRANCH_WIDTH, BRANCH_WIDTH,
              MLSTM_HEADS, MLSTM_HEADS, BRANCH_WIDTH, BRANCH_WIDTH, N_BRANCH * D_MODEL)
    offs = [0]
    for wd in widths:
        offs.append(offs[-1] + wd)
    seg = [w[:, offs[i]:offs[i + 1]] for i in range(len(widths))]
    (a_z, a_xbc, a_dt, b_q, b_f, b_i, b_g, c_q, c_k, c_v, c_o, c_i, c_f, d_x, d_g, gate) = seg
    main = jnp.concatenate([a_xbc, a_z, b_q, b_f, b_i, b_g, c_q, c_k, c_v, c_o, d_x, d_g, gate], axis=1)
    pad = jnp.zeros((w.shape[0], LANES - SSM_HEADS - 2 * MLSTM_HEADS), w.dtype)
    small = jnp.concatenate([a_dt, c_i, c_f, pad], axis=1)
    return main.astype(BF16), small.astype(BF16)


def _pad_lanes(v, start):
    out = jnp.zeros((1, LANES), F32)
    return out.at[0, start:start + v.shape[0]].set(v.astype(F32))


def kernel(x, p, mix_norm, w_in, ssm_conv_w, ssm_conv_b, ssm_dt_bias, ssm_a_log, ssm_d, ssm_norm, hgrn_lb_logits, hgrn_norm, mlstm_i_bias, mlstm_f_bias, mlstm_norm, lru_conv_w, lru_conv_b, lru_wa, lru_ba, lru_wi, lru_bi, lru_a_param, w_branch, w_out, mlp_norm, w_up, w_down, ple_norm, w_ple, w_ple_gate, final_norm):
    b, l, d = x.shape
    depth = w_in.shape[0]
    n = b * l
    row = lambda v: v.astype(F32).reshape(1, -1)
    e01 = (jnp.arange(LANES)[:, None] == (jnp.arange(BRANCH_WIDTH)[None, :] // SSM_HEAD_DIM)).astype(BF16)
    w_branch16, w_out16, w_up16, w_down16 = (w.astype(BF16) for w in (w_branch, w_out, w_up, w_down))
    w_ple16, w_ple_gate16, lru_wa16, lru_wi16 = (w.astype(BF16) for w in (w_ple, w_ple_gate, lru_wa, lru_wi))

    xf = x.reshape(n, d)
    for i in range(depth):
        w_main, w_small = _regroup_w_in(w_in[i])
        proj, small = _inproj(xf, row(mix_norm[i]), w_main, w_small, tm=1024, tn=1024)
        proj3 = proj.reshape(b, l, P_COLS)
        small3 = small.reshape(b, l, LANES)

        y_a = _ssd(proj3, small3, ssm_conv_w[i].astype(F32), row(ssm_conv_b[i]), _pad_lanes(ssm_dt_bias[i], SM_DT),
                   _pad_lanes(ssm_a_log[i], SM_DT), row(jnp.repeat(ssm_d[i], SSM_HEAD_DIM)), row(ssm_norm[i]), e01)
        y_b = _hgrn(proj3, hgrn_lb_logits.astype(F32), row(hgrn_norm[i]), i)
        gate_bias = _pad_lanes(jnp.concatenate([mlstm_i_bias[i], mlstm_f_bias[i]]), SM_CI)
        y_c = _mlstm(proj3, small3, gate_bias, row(mlstm_norm[i]))
        y_d = _lru(proj3, lru_conv_w[i].astype(F32), row(lru_conv_b[i]), lru_wa16[i], row(lru_ba[i]),
                   lru_wi16[i], row(lru_bi[i]), row(lru_a_param[i]))

        ys = [y.reshape(n, BRANCH_WIDTH) for y in (y_a, y_b, y_c, y_d)]
        merged = _merge(ys, proj, w_branch16[i], tm=1024, tn=512)
        xf = _mm_res(merged, w_out16[i], xf, tm=1024, tn=1024, tk=D_MODEL)

        up = _up(xf, row(mlp_norm[i]), w_up16[i], tm=1024, tn=1024)
        xf = _mm_res(up, w_down16[i], xf, tm=1024, tn=1024, tk=2048)

        ple_args = (xf, row(ple_norm[i]), w_ple_gate16[i], p[i].reshape(n, PLE_DIM), w_ple16[i])
        if i + 1 < depth:
            xf = _ple(*ple_args, tm=1024, tn=1024)
        else:
            xf = _ple_final(*ple_args, row(final_norm), tm=512, tn=1024)
    return xf.reshape(b, l, d)
```
